```python
import math
import jax, jax.numpy as jnp
from jax import lax
import numpy as np

D_MODEL = 1024
BATCH = 16
SEQ = 4096
DEPTH = 1
DEC_BATCH = 128
DEC_SEQ = 1
PAST_LEN = 8192
PAGE_SIZE = 128

HEAD_DIM = 64
N_HEADS = D_MODEL // HEAD_DIM
NSA_HEADS = N_HEADS // 2
RET_HEADS = N_HEADS - NSA_HEADS
KV_GROUPS = 2
HPG = NSA_HEADS // KV_GROUPS
CMP_BLK = 32
CMP_STRIDE = 16
CMP_HID = 4 * HEAD_DIM
SEL_BLK = 64
SEL_TOPN = 16
WINDOW = 512
Q_BLK = 128
RET_CHUNK = 128
D_FF = ((8 * D_MODEL + 3 * 256 - 1) // (3 * 256)) * 256
NSA_W = NSA_HEADS * HEAD_DIM
KV_W = KV_GROUPS * HEAD_DIM
GATE_W = NSA_HEADS * 3
RET_W = RET_HEADS * HEAD_DIM
IN_COLS = NSA_W + 6 * KV_W + GATE_W + 4 * RET_W
SPLITS = [NSA_W, NSA_W + 6 * KV_W, NSA_W + 6 * KV_W + GATE_W,
          NSA_W + 6 * KV_W + GATE_W + RET_W, NSA_W + 6 * KV_W + GATE_W + 2 * RET_W,
          NSA_W + 6 * KV_W + GATE_W + 3 * RET_W]
ROPE_THETA = 10000.0
RMS_EPS = 1e-6

kernel_name = 'hybrid_nsa_retention_decode_step'


def rmsnorm(x, g):
    xf = x.astype(jnp.float32)
    y = xf * lax.rsqrt(jnp.mean(xf * xf, axis=-1, keepdims=True) + RMS_EPS)
    return (y * g.astype(jnp.float32)).astype(x.dtype)


def rope(x, pos):
    half = HEAD_DIM // 2
    inv = 1.0 / (ROPE_THETA ** (jnp.arange(half, dtype=jnp.float32) / half))
    ang = pos.astype(jnp.float32)[:, None] * inv[None, :]
    cos = jnp.cos(ang)[:, None, :]
    sin = jnp.sin(ang)[:, None, :]
    xf = x.astype(jnp.float32)
    x1, x2 = xf[..., :half], xf[..., half:]
    return jnp.concatenate([x1 * cos - x2 * sin, x2 * cos + x1 * sin], axis=-1).astype(x.dtype)


def masked_softmax(s, mask):
    s = jnp.where(mask, s, -jnp.inf)
    m = jnp.max(s, axis=-1, keepdims=True)
    m = jnp.where(jnp.isfinite(m), m, 0.0)
    e = jnp.where(mask, jnp.exp(s - m), 0.0)
    return e / jnp.maximum(jnp.sum(e, axis=-1, keepdims=True), 1e-30)


def project(h, w_in, pos):
    B, T, _ = h.shape
    p = h @ w_in
    q, kv, gt, rq, rk, rv, rg = jnp.split(p, SPLITS, axis=-1)
    q = rope(q.reshape(B, T, NSA_HEADS, HEAD_DIM), pos)
    kv = kv.reshape(B, T, 3, 2, KV_GROUPS, HEAD_DIM)
    k = rope(kv[:, :, :, 0].reshape(B, T, 3 * KV_GROUPS, HEAD_DIM), pos).reshape(B, T, 3, KV_GROUPS, HEAD_DIM)
    v = kv[:, :, :, 1]
    rows = jnp.stack([k[:, :, 0], v[:, :, 0], k[:, :, 1], v[:, :, 1]], axis=2)
    win = jnp.stack([k[:, :, 2], v[:, :, 2]], axis=2)
    gates = jax.nn.sigmoid(gt.reshape(B, T, NSA_HEADS, 3))
    hs = (B, T, RET_HEADS, HEAD_DIM)
    rq = rope(rq.reshape(hs), pos)
    rk = rope(rk.reshape(hs), pos) * (HEAD_DIM ** -0.5)
    return q, gates, rows, win, rq, rk, rv.reshape(hs), rg.reshape(hs)


def compress(rows, pos_emb, w1, w2):
    L = rows.shape[0]
    nb = (L - CMP_BLK) // CMP_STRIDE + 1
    idx = np.arange(nb)[:, None] * CMP_STRIDE + np.arange(CMP_BLK)[None, :]
    blk = rows[idx] + pos_emb[None, :, None, :]
    blk = blk.transpose(0, 2, 1, 3).reshape(nb, KV_GROUPS, CMP_BLK * HEAD_DIM)
    return jax.nn.gelu(blk @ w1) @ w2


def block_cover(L):
    nb = (L - CMP_BLK) // CMP_STRIDE + 1
    nsel = -(-L // SEL_BLK)
    c0 = np.arange(nb) * CMP_STRIDE
    s0 = np.arange(nsel) * SEL_BLK
    m = (c0[None, :] < s0[:, None] + SEL_BLK) & (c0[None, :] + CMP_BLK > s0[:, None])
    return jnp.asarray(m.astype(np.float32))


def nsa_queries(q, gates, q_pos, kcb, vcb, cover, ks, vs, kw, vw, w_pos):
    Sq = q.shape[0]
    L = ks.shape[0]
    scale = HEAD_DIM ** -0.5
    qg = q.reshape(Sq, KV_GROUPS, HPG, HEAD_DIM)
    nb = kcb.shape[0]
    blk_end = jnp.asarray(np.arange(nb) * CMP_STRIDE + CMP_BLK - 1, jnp.int32)
    s_c = jnp.einsum('sghd,ngd->sghn', qg, kcb).astype(jnp.float32) * scale
    p_c = masked_softmax(s_c, (blk_end[None, :] <= q_pos[:, None])[:, None, None, :])
    o_c = jnp.einsum('sghn,ngd->sghd', p_c.astype(vcb.dtype), vcb)
    imp = jnp.einsum('sghn,jn->sgj', p_c, cover)
    nsel = cover.shape[0]
    j = jnp.arange(nsel)[None, :]
    cur = (q_pos // SEL_BLK)[:, None]
    forced = (j == 0) | (j == cur) | (j == cur - 1)
    valid = j * SEL_BLK <= q_pos[:, None]
    score = jnp.where(forced[:, None], jnp.inf, jnp.where(valid[:, None], imp, -jnp.inf))
    n_top = min(SEL_TOPN, nsel)
    _, idx = lax.top_k(score, n_top)
    tok = (idx[..., None] * SEL_BLK + jnp.arange(SEL_BLK)).reshape(Sq, KV_GROUPS, n_top * SEL_BLK)
    tok_c = jnp.minimum(tok, L - 1)
    g_idx = jnp.arange(KV_GROUPS)[None, :, None]
    k_g = ks[tok_c, g_idx]
    v_g = vs[tok_c, g_idx]
    s_s = jnp.einsum('sghd,sgkd->sghk', qg, k_g).astype(jnp.float32) * scale
    p_s = masked_softmax(s_s, (tok <= q_pos[:, None, None])[:, :, None, :])
    o_s = jnp.einsum('sghk,sgkd->sghd', p_s.astype(v_g.dtype), v_g)
    wmask = (w_pos[None, :] <= q_pos[:, None]) & (w_pos[None, :] > q_pos[:, None] - WINDOW) & (w_pos[None, :] >= 0)
    s_w = jnp.einsum('sghd,wgd->sghw', qg, kw).astype(jnp.float32) * scale
    p_w = masked_softmax(s_w, wmask[:, None, None, :])
    o_w = jnp.einsum('sghw,wgd->sghd', p_w.astype(vw.dtype), vw)
    g = gates.reshape(Sq, KV_GROUPS, HPG, 3).astype(o_c.dtype)
    o = g[..., 0:1] * o_c + g[..., 1:2] * o_s + g[..., 2:3] * o_w
    return o.reshape(Sq, NSA_W)


def nsa_prompt_seq(q, gt, rows, win, cw):
    pk, w1k, w2k, pv, w1v, w2v = cw
    L = q.shape[0]
    kcb = compress(rows[:, 0], pk, w1k, w2k)
    vcb = compress(rows[:, 1], pv, w1v, w2v)
    cover = block_cover(L)
    win_pad = jnp.concatenate([jnp.zeros((WINDOW,) + win.shape[1:], win.dtype), win], axis=0)

    def qblock(i):
        start = i * Q_BLK
        qb = lax.dynamic_slice_in_dim(q, start, Q_BLK, 0)
        gb = lax.dynamic_slice_in_dim(gt, start, Q_BLK, 0)
        wb = lax.dynamic_slice_in_dim(win_pad, start, WINDOW + Q_BLK, 0)
        q_pos = start + jnp.arange(Q_BLK)
        w_pos = start - WINDOW + jnp.arange(WINDOW + Q_BLK)
        return nsa_queries(qb, gb, q_pos, kcb, vcb, cover, rows[:, 2], rows[:, 3], wb[:, 0], wb[:, 1], w_pos)

    return lax.map(qblock, jnp.arange(L // Q_BLK)).reshape(L, NSA_W)


def nsa_sample_seq(q, gt, rows, win_cat, pages, pool, q_pos, w_pos, cw):
    pk, w1k, w2k, pv, w1v, w2v = cw
    past = pool[pages]
    past = past.reshape((-1,) + past.shape[2:])
    full = jnp.concatenate([past, rows], axis=0)
    kcb = compress(full[:, 0], pk, w1k, w2k)
    vcb = compress(full[:, 1], pv, w1v, w2v)
    cover = block_cover(full.shape[0])
    return nsa_queries(q, gt, q_pos, kcb, vcb, cover, full[:, 2], full[:, 3], win_cat[:, 0], win_cat[:, 1], w_pos)


def log_gamma():
    return jnp.log(1.0 - 2.0 ** (-5.0 - jnp.arange(RET_HEADS, dtype=jnp.float32)))


def retention_chunk(state, q, k, v, lg):
    C = q.shape[1]
    i = jnp.arange(C, dtype=jnp.float32)
    diff = i[:, None] - i[None, :]
    causal = diff >= 0
    decay = jnp.where(causal[None], jnp.exp(jnp.where(causal, diff, 0.0)[None] * lg[:, None, None]), 0.0)
    att = jnp.einsum('bihd,bjhd->bhij', q, k) * decay[None]
    o = jnp.einsum('bhij,bjhe->bihe', att, v)
    o = o + jnp.einsum('bihd,bhde->bihe', q, state) * jnp.exp((i + 1.0)[:, None] * lg[None, :])[None, :, :, None]
    k_dec = k * jnp.exp((C - 1.0 - i)[:, None] * lg[None, :])[None, :, :, None]
    state = state * jnp.exp(C * lg)[None, :, None, None] + jnp.einsum('bjhd,bjhe->bhde', k_dec, v)
    return state, o


def retention_prompt(q, k, v, lg):
    B, T, H, d = q.shape
    n = T // RET_CHUNK

    def to_chunks(a):
        return a.reshape(B, n, RET_CHUNK, H, d).transpose(1, 0, 2, 3, 4)

    s0 = jnp.zeros((B, H, d, d), jnp.float32)
    state, o = lax.scan(lambda s, c: retention_chunk(s, c[0], c[1], c[2], lg), s0,
                        (to_chunks(q), to_chunks(k), to_chunks(v)))
    return state, o.transpose(1, 0, 2, 3, 4).reshape(B, T, H, d)


def retention_out(o, g, dtype):
    o = o * lax.rsqrt(jnp.mean(o * o, axis=-1, keepdims=True) + RMS_EPS)
    o = o * jax.nn.silu(g.astype(jnp.float32))
    B, T = o.shape[:2]
    return o.reshape(B, T, RET_W).astype(dtype)


def swiglu(h, wg, wu, wd):
    return (jax.nn.silu(h @ wg) * (h @ wu)) @ wd


def setup_inputs(seed: int = 0) -> dict:
    key = jax.random.key(seed)
    ks = jax.random.split(key, 24)
    n_pages = PAST_LEN // PAGE_SIZE
    n_pool = (DEC_BATCH * n_pages * 5) // 4
    wbuf = min(WINDOW, PAST_LEN)
    f32 = jnp.float32

    def nrm(k, shape, s=1.0):
        return jax.random.normal(k, shape, f32) * s

    page_table = jax.random.permutation(ks[5], n_pool)[: DEC_BATCH * n_pages].reshape(DEC_BATCH, n_pages).astype(jnp.int32)
    return {
        'x_prompt': nrm(ks[0], (BATCH, SEQ, D_MODEL)),
        'x_sample': nrm(ks[1], (DEC_BATCH, DEC_SEQ, D_MODEL)),
        'cache_kv': nrm(ks[2], (DEPTH, n_pool, PAGE_SIZE, 4, KV_GROUPS, HEAD_DIM)),
        'cache_win': nrm(ks[3], (DEPTH, DEC_BATCH, wbuf, 2, KV_GROUPS, HEAD_DIM)),
        'state_ret': nrm(ks[4], (DEPTH, DEC_BATCH, RET_HEADS, HEAD_DIM, HEAD_DIM)),
        'page_table': page_table,
        'ln1': 1.0 + nrm(ks[6], (DEPTH, D_MODEL), 0.02),
        'w_in': nrm(ks[7], (DEPTH, D_MODEL, IN_COLS), D_MODEL ** -0.5),
        'cmp_pos_k': nrm(ks[8], (DEPTH, CMP_BLK, HEAD_DIM), 0.1),
        'cmp_w1_k': nrm(ks[9], (DEPTH, CMP_BLK * HEAD_DIM, CMP_HID), (CMP_BLK * HEAD_DIM) ** -0.5),
        'cmp_w2_k': nrm(ks[10], (DEPTH, CMP_HID, HEAD_DIM), CMP_HID ** -0.5),
        'cmp_pos_v': nrm(ks[11], (DEPTH, CMP_BLK, HEAD_DIM), 0.1),
        'cmp_w1_v': nrm(ks[12], (DEPTH, CMP_BLK * HEAD_DIM, CMP_HID), (CMP_BLK * HEAD_DIM) ** -0.5),
        'cmp_w2_v': nrm(ks[13], (DEPTH, CMP_HID, HEAD_DIM), CMP_HID ** -0.5),
        'w_out': nrm(ks[14], (DEPTH, D_MODEL, D_MODEL), D_MODEL ** -0.5),
        'ln2': 1.0 + nrm(ks[15], (DEPTH, D_MODEL), 0.02),
        'w_gate': nrm(ks[16], (DEPTH, D_MODEL, D_FF), D_MODEL ** -0.5),
        'w_up': nrm(ks[17], (DEPTH, D_MODEL, D_FF), D_MODEL ** -0.5),
        'w_down': nrm(ks[18], (DEPTH, D_FF, D_MODEL), D_FF ** -0.5),
        'ln_f': 1.0 + nrm(ks[19], (D_MODEL,), 0.02),
    }


def reference(x_prompt, x_sample, cache_kv, cache_win, state_ret, page_table, ln1, w_in,
              cmp_pos_k, cmp_w1_k, cmp_w2_k, cmp_pos_v, cmp_w1_v, cmp_w2_v, w_out, ln2,
              w_gate, w_up, w_down, ln_f):
    f32 = jnp.float32
    lg = log_gamma()
    past_len = page_table.shape[1] * PAGE_SIZE
    s_p = x_prompt.shape[1]
    s_s = x_sample.shape[1]
    pos_p = jnp.arange(s_p)
    pos_s = past_len + jnp.arange(s_s)
    win_keep = min(WINDOW, s_p)
    wbuf = cache_win.shape[2]
    w_pos_s = past_len - wbuf + jnp.arange(wbuf + s_s)
    xp, xs = x_prompt, x_sample
    kv_p, kv_s, win_p, win_s, ret_p, ret_s = [], [], [], [], [], []
    for l in range(DEPTH):
        cw = (cmp_pos_k[l], cmp_w1_k[l], cmp_w2_k[l], cmp_pos_v[l], cmp_w1_v[l], cmp_w2_v[l])
        hp = rmsnorm(xp, ln1[l])
        q, gt, rows, win, rq, rk, rv, rg = project(hp, w_in[l], pos_p)
        o_nsa = lax.map(lambda a: nsa_prompt_seq(a[0], a[1], a[2], a[3], cw), (q, gt, rows, win))
        st, o_r = retention_prompt(rq.astype(f32), rk.astype(f32), rv.astype(f32), lg)
        mix = jnp.concatenate([o_nsa, retention_out(o_r, rg, xp.dtype)], axis=-1) @ w_out[l]
        xp = xp + mix
        xp = xp + swiglu(rmsnorm(xp, ln2[l]), w_gate[l], w_up[l], w_down[l])
        kv_p.append(rows)
        win_p.append(win[:, s_p - win_keep:])
        ret_p.append(st.astype(x_prompt.dtype))
        hs = rmsnorm(xs, ln1[l])
        q, gt, rows, win, rq, rk, rv, rg = project(hs, w_in[l], pos_s)
        pool = cache_kv[l]
        win_cat = jnp.concatenate([cache_win[l], win], axis=1)
        o_nsa = lax.map(lambda a: nsa_sample_seq(a[0], a[1], a[2], a[3], a[4], pool, pos_s, w_pos_s, cw),
                        (q, gt, rows, win_cat, page_table))
        st, o_r = retention_chunk(state_ret[l].astype(f32), rq.astype(f32), rk.astype(f32), rv.astype(f32), lg)
        mix = jnp.concatenate([o_nsa, retention_out(o_r, rg, xs.dtype)], axis=-1) @ w_out[l]
        xs = xs + mix
        xs = xs + swiglu(rmsnorm(xs, ln2[l]), w_gate[l], w_up[l], w_down[l])
        kv_s.append(rows)
        win_s.append(win_cat[:, s_s:])
        ret_s.append(st.astype(state_ret.dtype))
    y_prompt = rmsnorm(xp, ln_f)
    y_sample = rmsnorm(xs, ln_f)
    return (y_prompt, y_sample, jnp.stack(kv_p), jnp.stack(kv_s), jnp.stack(win_p), jnp.stack(win_s),
            jnp.stack(ret_p), jnp.stack(ret_s))
```

```python
import functools

import numpy as np
import jax
import jax.numpy as jnp
from jax import lax
from jax.experimental import pallas as pl
from jax.experimental.pallas import tpu as pltpu

HEAD_DIM = 64
NSA_HEADS = 8
RET_HEADS = 8
KV_GROUPS = 2
HPG = NSA_HEADS // KV_GROUPS
CMP_BLK = 32
CMP_STRIDE = 16
CMP_HID = 4 * HEAD_DIM
SEL_BLK = 64
SEL_TOPN = 16
WINDOW = 512
Q_BLK = 128
RET_CHUNK = 128
PAGE_SIZE = 128
ROPE_THETA = 10000.0
RMS_EPS = 1e-6

LANES = 128
NSA_W = NSA_HEADS * HEAD_DIM
RET_W = RET_HEADS * HEAD_DIM
KV_W = KV_GROUPS * HEAD_DIM
NEG_BIG = -(2.0 ** 100)
VMEM_LIMIT = 56 * 1024 * 1024

_MXU = jnp.bfloat16
_F32 = jnp.float32


def _dot(a, b):
    return jnp.dot(a.astype(_MXU), b.astype(_MXU), preferred_element_type=_F32)


def _dot_nt(a, b):
    return lax.dot_general(a.astype(_MXU), b.astype(_MXU), (((1,), (1,)), ((), ())),
                           preferred_element_type=_F32)


def _dot_tn(a, b):
    return lax.dot_general(a.astype(_MXU), b.astype(_MXU), (((0,), (0,)), ((), ())),
                           preferred_element_type=_F32)


def _split3(x):
    hi = x.astype(_MXU)
    r1 = x - hi.astype(_F32)
    mid = r1.astype(_MXU)
    lo = (r1 - mid.astype(_F32)).astype(_MXU)
    return hi, mid, lo


def _params(sem):
    return pltpu.CompilerParams(dimension_semantics=sem, vmem_limit_bytes=VMEM_LIMIT)


_C_Q, _C_KV, _C_RQ, _C_RK, _C_RV, _C_RG, _C_GT = 0, 512, 1280, 1792, 2304, 2816, 3328
_PROJ_COLS = 3456


def _proj_kernel(x_ref, g_ref, w_ref, cos_ref, sa_ref, sb_ref,
                 q_ref, rows_ref, win_ref, kvb_ref, gates_ref, ret_ref):
    x = x_ref[...]
    ms = jnp.mean(x * x, axis=-1, keepdims=True)
    h = (x * lax.rsqrt(ms + RMS_EPS) * g_ref[...]).astype(_MXU)
    cos, sa, sb = cos_ref[...], sa_ref[...], sb_ref[...]

    def seg(c0, n):
        return jnp.dot(h, w_ref[:, c0:c0 + n], preferred_element_type=_F32)

    def rope(p):
        return p * cos + pltpu.roll(p, LANES - 32, 1) * sa + pltpu.roll(p, 32, 1) * sb

    def slab(p, s):
        return p[:, s * LANES:(s + 1) * LANES]

    scale = HEAD_DIM ** -0.5
    p = seg(_C_Q, NSA_W)
    for s in range(4):
        q_ref[:, s * LANES:(s + 1) * LANES] = (rope(slab(p, s)) * scale).astype(q_ref.dtype)
    p = seg(_C_KV, 6 * KV_W)
    kc, vc = rope(slab(p, 0)), slab(p, 1)
    ks, vs = rope(slab(p, 2)), slab(p, 3)
    kw, vw = rope(slab(p, 4)), slab(p, 5)
    rows_ref[:, 0:128] = kc
    rows_ref[:, 128:256] = vc
    rows_ref[:, 256:384] = ks
    rows_ref[:, 384:512] = vs
    win_ref[:, 0:128] = kw
    win_ref[:, 128:256] = vw
    kvb_ref[:, 0:128] = ks.astype(kvb_ref.dtype)
    kvb_ref[:, 128:256] = vs.astype(kvb_ref.dtype)
    kvb_ref[:, 256:384] = kw.astype(kvb_ref.dtype)
    kvb_ref[:, 384:512] = vw.astype(kvb_ref.dtype)
    p = seg(_C_RQ, RET_W)
    for s in range(4):
        ret_ref[:, s * LANES:(s + 1) * LANES] = rope(slab(p, s))
    p = seg(_C_RK, RET_W)
    for s in range(4):
        ret_ref[:, RET_W + s * LANES:RET_W + (s + 1) * LANES] = rope(slab(p, s)) * scale
    ret_ref[:, 2 * RET_W:3 * RET_W] = seg(_C_RV, RET_W)
    ret_ref[:, 3 * RET_W:4 * RET_W] = seg(_C_RG, RET_W)
    gates_ref[...] = jax.nn.sigmoid(seg(_C_GT, LANES))


def _proj(x2, ln, w_perm, cos, sa, sb, tm):
    n, d = x2.shape
    tt = cos.shape[0]
    nt = tt // tm
    row = lambda i: (i, 0)
    tab = lambda i: (i % nt, 0)
    const = lambda i: (0, 0)
    out_shape = (
        jax.ShapeDtypeStruct((n, NSA_W), _MXU),
        jax.ShapeDtypeStruct((n, 4 * KV_W), _F32),
        jax.ShapeDtypeStruct((n, 2 * KV_W), _F32),
        jax.ShapeDtypeStruct((n, 4 * KV_W), _MXU),
        jax.ShapeDtypeStruct((n, LANES), _F32),
        jax.ShapeDtypeStruct((n, 4 * RET_W), _F32),
    )
    return pl.pallas_call(
        _proj_kernel,
        grid=(n // tm,),
        in_specs=[
            pl.BlockSpec((tm, d), row),
            pl.BlockSpec((1, d), const),
            pl.BlockSpec((d, _PROJ_COLS), const),
            pl.BlockSpec((tm, LANES), tab),
            pl.BlockSpec((tm, LANES), tab),
            pl.BlockSpec((tm, LANES), tab),
        ],
        out_specs=(
            pl.BlockSpec((tm, NSA_W), row),
            pl.BlockSpec((tm, 4 * KV_W), row),
            pl.BlockSpec((tm, 2 * KV_W), row),
            pl.BlockSpec((tm, 4 * KV_W), row),
            pl.BlockSpec((tm, LANES), row),
            pl.BlockSpec((tm, 4 * RET_W), row),
        ),
        out_shape=out_shape,
        compiler_params=_params(("parallel",)),
        name="proj",
    )(x2, ln, w_perm, cos, sa, sb)


def _post_kernel(x_ref, on_ref, or_ref, wo_ref, g2_ref, wg_ref, wu_ref, wd_ref, gf_ref, y_ref):
    x = x_ref[...]
    mix = (jnp.dot(on_ref[...], wo_ref[0:NSA_W, :], preferred_element_type=_F32)
           + jnp.dot(or_ref[...], wo_ref[NSA_W:NSA_W + RET_W, :], preferred_element_type=_F32))
    x1 = x + mix
    ms = jnp.mean(x1 * x1, axis=-1, keepdims=True)
    h = (x1 * lax.rsqrt(ms + RMS_EPS) * g2_ref[...]).astype(_MXU)
    a = jax.nn.silu(jnp.dot(h, wg_ref[...], preferred_element_type=_F32))
    a = a * jnp.dot(h, wu_ref[...], preferred_element_type=_F32)
    y = x1 + jnp.dot(a.astype(_MXU), wd_ref[...], preferred_element_type=_F32)
    ms = jnp.mean(y * y, axis=-1, keepdims=True)
    y_ref[...] = y * lax.rsqrt(ms + RMS_EPS) * gf_ref[...]


def _post(x2, o_nsa, o_ret, w_out, ln2, w_gate, w_up, w_down, ln_f, tm):
    n, d = x2.shape
    dff = w_gate.shape[1]
    row = lambda i: (i, 0)
    const = lambda i: (0, 0)
    once = dict(pipeline_mode=pl.Buffered(1))
    return pl.pallas_call(
        _post_kernel,
        grid=(n // tm,),
        in_specs=[
            pl.BlockSpec((tm, d), row),
            pl.BlockSpec((tm, NSA_W), row),
            pl.BlockSpec((tm, RET_W), row),
            pl.BlockSpec((d, d), const, **once),
            pl.BlockSpec((1, d), const),
            pl.BlockSpec((d, dff), const, **once),
            pl.BlockSpec((d, dff), const, **once),
            pl.BlockSpec((dff, d), const, **once),
            pl.BlockSpec((1, d), const),
        ],
        out_specs=pl.BlockSpec((tm, d), row),
        out_shape=jax.ShapeDtypeStruct((n, d), _F32),
        compiler_params=_params(("parallel",)),
        name="post",
    )(x2, o_nsa, o_ret, w_out, ln2, w_gate, w_up, w_down, ln_f)


def _compress_body(src_refs, nb16, pos_ref, w1_ref, w2_ref):
    outs = []
    for c in range(2):
        a = jnp.zeros((nb16, 2 * CMP_HID), _F32)
        b = jnp.zeros((nb16, 2 * CMP_HID), _F32)
        for t in range(CMP_STRIDE):
            xt = src_refs[c][pl.ds(t, nb16, stride=CMP_STRIDE), :]
            a = a + _dot(xt + pos_ref[c, t:t + 1, :], w1_ref[c, t])
            b = b + _dot(xt + pos_ref[c, CMP_STRIDE + t:CMP_STRIDE + t + 1, :], w1_ref[c, CMP_STRIDE + t])
        hid = a + pltpu.roll(b, nb16 - 1, 0)
        outs.append(_dot(jax.nn.gelu(hid), w2_ref[c]))
    return outs


def _compress_prompt_kernel(kc_ref, vc_ref, pos_ref, w1_ref, w2_ref, kcb_ref, vcb_ref):
    nb16 = kcb_ref.shape[0]
    kcb, vcb = _compress_body((kc_ref, vc_ref), nb16, pos_ref, w1_ref, w2_ref)
    kcb_ref[...] = kcb
    vcb_ref[...] = vcb


def _compress_prompt(rows3, pos, w1bd, w2bd):
    b, t, _ = rows3.shape
    nb16 = t // CMP_STRIDE
    c3 = lambda i: (0, 0, 0)
    c4 = lambda i: (0, 0, 0, 0)
    out = jax.ShapeDtypeStruct((b, nb16, LANES), _F32)
    return pl.pallas_call(
        _compress_prompt_kernel,
        grid=(b,),
        in_specs=[
            pl.BlockSpec((None, t, LANES), lambda i: (i, 0, 0)),
            pl.BlockSpec((None, t, LANES), lambda i: (i, 0, 1)),
            pl.BlockSpec(pos.shape, c3),
            pl.BlockSpec(w1bd.shape, c4),
            pl.BlockSpec(w2bd.shape, c3),
        ],
        out_specs=(pl.BlockSpec((None, nb16, LANES), lambda i: (i, 0, 0)),
                   pl.BlockSpec((None, nb16, LANES), lambda i: (i, 0, 0))),
        out_shape=(out, out),
        compiler_params=_params(("parallel",)),
        name="compress_prompt",
    )(rows3, rows3, pos, w1bd, w2bd)


def _compress_sample_kernel(n_pages, pt_ref, *refs):
    page_refs = refs[:n_pages]
    pos_ref, w1_ref, w2_ref, kcb_ref, vcb_ref, buf_ref = refs[n_pages:]
    for j in range(n_pages):
        for c in range(2):
            buf_ref[c, j * PAGE_SIZE:(j + 1) * PAGE_SIZE, :] = page_refs[j][:, c * LANES:(c + 1) * LANES]
    nb16 = kcb_ref.shape[0]
    kcb, vcb = _compress_body((buf_ref.at[0], buf_ref.at[1]), nb16, pos_ref, w1_ref, w2_ref)
    kcb_ref[...] = kcb
    vcb_ref[...] = vcb


def _compress_sample(pool3, page_table_flat, b2, n_pages, pos, w1bd, w2bd):
    past = n_pages * PAGE_SIZE
    nb16 = past // CMP_STRIDE
    c3 = lambda i, pt: (0, 0, 0)
    c4 = lambda i, pt: (0, 0, 0, 0)

    def page_spec(j):
        return pl.BlockSpec((None, PAGE_SIZE, 2 * LANES), lambda i, pt: (pt[i * n_pages + j], 0, 0))

    out = jax.ShapeDtypeStruct((b2, nb16, LANES), _F32)
    grid_spec = pltpu.PrefetchScalarGridSpec(
        num_scalar_prefetch=1,
        grid=(b2,),
        in_specs=[page_spec(j) for j in range(n_pages)] + [
            pl.BlockSpec(pos.shape, c3),
            pl.BlockSpec(w1bd.shape, c4),
            pl.BlockSpec(w2bd.shape, c3),
        ],
        out_specs=(pl.BlockSpec((None, nb16, LANES), lambda i, pt: (i, 0, 0)),
                   pl.BlockSpec((None, nb16, LANES), lambda i, pt: (i, 0, 0))),
        scratch_shapes=[pltpu.VMEM((2, past, LANES), _F32)],
    )
    return pl.pallas_call(
        functools.partial(_compress_sample_kernel, n_pages),
        grid_spec=grid_spec,
        out_shape=(out, out),
        compiler_params=_params(("arbitrary",)),
        name="compress_sample",
    )(page_table_flat, *([pool3] * n_pages), pos, w1bd, w2bd)


def _group_queries(qf, rows):
    lane = lax.broadcasted_iota(jnp.int32, (rows, LANES), 1)
    out = []
    for h in range(NSA_HEADS):
        g = h // HPG
        s = qf[:, (h // 2) * LANES:(h // 2 + 1) * LANES]
        if (h % 2) != g:
            s = pltpu.roll(s, HEAD_DIM, 1)
        out.append(jnp.where((lane >= g * HEAD_DIM) & (lane < (g + 1) * HEAD_DIM), s, 0.0))
    return out


def _masked_softmax(s, mask):
    s = jnp.where(mask, s, -jnp.inf)
    m = jnp.max(s, axis=-1, keepdims=True)
    m = jnp.where(m > -jnp.inf, m, 0.0)
    e = jnp.where(mask, jnp.exp(s - m), 0.0)
    return e / jnp.maximum(jnp.sum(e, axis=-1, keepdims=True), 1e-30)


def _place_heads(o_heads, rows):
    lane = lax.broadcasted_iota(jnp.int32, (rows, LANES), 1)
    slabs = []
    for k in range(NSA_HEADS // 2):
        pair = []
        for h in (2 * k, 2 * k + 1):
            g = h // HPG
            o = o_heads[h]
            if (h % 2) != g:
                o = pltpu.roll(o, HEAD_DIM, 1)
            pair.append(o)
        slabs.append(jnp.where(lane < HEAD_DIM, pair[0], pair[1]))
    return slabs


def _nsa_prompt_kernel(nsel, q_ref, gates_ref, kcb_ref, vcb_ref, kvb_ref, eoh_ref, cover_ref,
                       o_ref, acc_ref, m_ref, l_ref):
    i = pl.program_id(1)
    tq = Q_BLK
    rows_all = NSA_HEADS * tq
    qf = q_ref[...].astype(_F32)
    q_heads = _group_queries(qf, tq)
    q2 = jnp.concatenate(q_heads, axis=0).astype(_MXU)
    qpos = i * tq + lax.broadcasted_iota(jnp.int32, (tq, 1), 0)

    nb16 = kcb_ref.shape[0]
    s_c = _dot_nt(q2, kcb_ref[...]).reshape(NSA_HEADS, tq, nb16)
    blk_end = lax.broadcasted_iota(jnp.int32, (tq, nb16), 1) * CMP_STRIDE + (CMP_BLK - 1)
    p_c = _masked_softmax(s_c, (blk_end <= qpos)[None]).reshape(rows_all, nb16)
    o_c = _dot(p_c, vcb_ref[...])

    jidx = lax.broadcasted_iota(jnp.int32, (nsel, tq), 0)
    qpos_l = i * tq + lax.broadcasted_iota(jnp.int32, (nsel, tq), 1)
    cur = qpos_l // SEL_BLK
    forced = (jidx == 0) | (jidx == cur) | (jidx == cur - 1)
    valid = jidx * SEL_BLK <= qpos_l
    n_top = min(SEL_TOPN, nsel)
    sub = lax.broadcasted_iota(jnp.int32, (8, tq), 0)
    q_sel = []
    for g in range(KV_GROUPS):
        psum = p_c[(g * HPG) * tq:(g * HPG + 1) * tq]
        for hh in range(1, HPG):
            psum = psum + p_c[(g * HPG + hh) * tq:(g * HPG + hh + 1) * tq]
        imp_t = jnp.zeros((nsel, tq), _F32)
        for part in _split3(psum):
            imp_t = imp_t + lax.dot_general(cover_ref[...], part, (((1,), (1,)), ((), ())),
                                            preferred_element_type=_F32)
        score = jnp.where(forced, jnp.inf, jnp.where(valid, imp_t, -jnp.inf))
        nv = nsel // 8
        sc_v = [score[8 * v:8 * v + 8] for v in range(nv)]
        rank_v = [jnp.zeros((8, tq), jnp.int32) for _ in range(nv)]
        for jp in range(nsel):
            rowb = jnp.broadcast_to(score[jp:jp + 1, :], (8, tq))
            for v in range(nv):
                if v > jp // 8:
                    beats = jnp.where(rowb >= sc_v[v], 1, 0)
                elif v < jp // 8:
                    beats = jnp.where(rowb > sc_v[v], 1, 0)
                else:
                    beats = jnp.where(sub > (jp % 8), jnp.where(rowb >= sc_v[v], 1, 0),
                                      jnp.where(rowb > sc_v[v], 1, 0))
                rank_v[v] = rank_v[v] + beats
        sel_t = jnp.concatenate([jnp.where(r < n_top, 1.0, 0.0) for r in rank_v], axis=0)
        if nsel < LANES:
            sel_t = jnp.concatenate([sel_t, jnp.ones((LANES - nsel, tq), _F32)], axis=0)
        sel = sel_t.T
        bias = ((sel - 1.0) * (-NEG_BIG)).astype(_MXU)
        q_sel.extend([bias] * HPG)
    q_sel = jnp.concatenate(q_sel, axis=0)
    q_aug = jnp.concatenate([q2, q_sel], axis=1)

    tk = 2 * tq
    m_ref[...] = jnp.full(m_ref.shape, 2.0 * NEG_BIG, _F32)
    l_ref[...] = jnp.zeros(l_ref.shape, _F32)
    acc_ref[...] = jnp.zeros(acc_ref.shape, _F32)
    qpos_all = i * tq + (lax.broadcasted_iota(jnp.int32, (rows_all, 1), 0) & (tq - 1))

    def key_tile(kt, causal):
        k0 = pl.multiple_of(kt * tk, tk)
        kk = kvb_ref[pl.ds(k0, tk), 0:LANES]
        vv = kvb_ref[pl.ds(k0, tk), LANES:2 * LANES]
        k_aug = jnp.concatenate([kk, eoh_ref[pl.ds(k0, tk), :]], axis=1)
        s = _dot_nt(q_aug, k_aug)
        if causal:
            kpos = k0 + lax.broadcasted_iota(jnp.int32, (rows_all, tk), 1)
            s = jnp.where(kpos <= qpos_all, s, NEG_BIG)
        m_old = m_ref[...]
        m_new = jnp.maximum(m_old, jnp.max(s, axis=-1, keepdims=True))
        alpha = jnp.exp(m_old - m_new)
        p = jnp.exp(s - m_new)
        l_ref[...] = alpha * l_ref[...] + jnp.sum(p, axis=-1, keepdims=True)
        acc_ref[...] = alpha * acc_ref[...] + _dot(p, vv)
        m_ref[...] = m_new

    nt = (i + 2) // 2

    def body(kt, carry):
        key_tile(kt, False)
        return carry

    lax.fori_loop(0, nt - 1, body, 0)
    key_tile(nt - 1, True)
    o_s = acc_ref[...] / l_ref[...]

    wk = WINDOW + tq
    start = pl.multiple_of(jnp.maximum(i * tq - WINDOW, 0), tq)
    kw = kvb_ref[pl.ds(start, wk), 2 * LANES:3 * LANES]
    vw = kvb_ref[pl.ds(start, wk), 3 * LANES:4 * LANES]
    s_w = _dot_nt(q2, kw).reshape(NSA_HEADS, tq, wk)
    wpos = start + lax.broadcasted_iota(jnp.int32, (tq, wk), 1)
    wmask = (wpos <= qpos) & (wpos > qpos - WINDOW)
    p_w = _masked_softmax(s_w, wmask[None]).reshape(rows_all, wk)
    o_w = _dot(p_w, vw)

    gt = gates_ref[...]
    o_heads = []
    for h in range(NSA_HEADS):
        r = slice(h * tq, (h + 1) * tq)
        o_heads.append(gt[:, 3 * h:3 * h + 1] * o_c[r] + gt[:, 3 * h + 1:3 * h + 2] * o_s[r]
                       + gt[:, 3 * h + 2:3 * h + 3] * o_w[r])
    for k, s in enumerate(_place_heads(o_heads, tq)):
        o_ref[:, k * LANES:(k + 1) * LANES] = s.astype(o_ref.dtype)


def _nsa_prompt(q, gates, kcb, vcb, kvb3, eoh, cover, b, t):
    nq = t // Q_BLK
    nb16 = t // CMP_STRIDE
    nsel = t // SEL_BLK
    rowblk = lambda bi, i: (bi * nq + i, 0)
    per_b = lambda bi, i: (bi, 0, 0)
    const = lambda bi, i: (0, 0)
    return pl.pallas_call(
        functools.partial(_nsa_prompt_kernel, nsel),
        grid=(b, nq),
        in_specs=[
            pl.BlockSpec((Q_BLK, NSA_W), rowblk),
            pl.BlockSpec((Q_BLK, LANES), rowblk),
            pl.BlockSpec((None, nb16, LANES), per_b),
            pl.BlockSpec((None, nb16, LANES), per_b),
            pl.BlockSpec((None, t, 4 * KV_W), per_b),
            pl.BlockSpec((t, LANES), const),
            pl.BlockSpec((nsel, nb16), const),
        ],
        out_specs=pl.BlockSpec((Q_BLK, NSA_W), rowblk),
        out_shape=jax.ShapeDtypeStruct((b * t, NSA_W), _MXU),
        scratch_shapes=[pltpu.VMEM((NSA_HEADS * Q_BLK, LANES), _F32),
                        pltpu.VMEM((NSA_HEADS * Q_BLK, 1), _F32),
                        pltpu.VMEM((NSA_HEADS * Q_BLK, 1), _F32)],
        compiler_params=_params(("parallel", "arbitrary")),
        name="nsa_prompt",
    )(q, gates, kcb, vcb, kvb3, eoh, cover)


def _ret_prompt_kernel(ret_ref, dmat_ref, rowdec_ref, kdec_ref, sdec_ref, bmask_ref,
                       o_ref, st_ref, s_ref):
    c = pl.program_id(1)
    nc = pl.num_programs(1)
    tq = RET_CHUNK

    @pl.when(c == 0)
    def _():
        s_ref[...] = jnp.zeros(s_ref.shape, _F32)

    lane = lax.broadcasted_iota(jnp.int32, (tq, LANES), 1)
    lo = lane < HEAD_DIM
    bmask = bmask_ref[...]
    for pr in range(RET_HEADS // 2):
        c0 = pr * LANES
        q = ret_ref[:, c0:c0 + LANES]
        k = ret_ref[:, RET_W + c0:RET_W + c0 + LANES]
        v = ret_ref[:, 2 * RET_W + c0:2 * RET_W + c0 + LANES]
        g = ret_ref[:, 3 * RET_W + c0:3 * RET_W + c0 + LANES]
        q2 = jnp.concatenate([jnp.where(lo, q, 0.0), jnp.where(lo, 0.0, q)], axis=0)
        att = _dot_nt(q2, k) * jnp.concatenate([dmat_ref[2 * pr], dmat_ref[2 * pr + 1]], axis=0)
        o2 = _dot(att, v)
        o = jnp.where(lo, o2[0:tq], o2[tq:2 * tq])
        s_old = s_ref[pr]
        o = o + _dot(q, s_old) * rowdec_ref[pr]
        s_new = s_old * sdec_ref[pr] + _dot_tn(k * kdec_ref[pr], v) * bmask
        s_ref[pr] = s_new
        o_sq = o * o
        s0 = jnp.sum(jnp.where(lo, o_sq, 0.0), axis=-1, keepdims=True)
        s1 = jnp.sum(jnp.where(lo, 0.0, o_sq), axis=-1, keepdims=True)
        ms = jnp.where(lo, s0, s1) * (1.0 / HEAD_DIM)
        o = o * lax.rsqrt(ms + RMS_EPS) * jax.nn.silu(g)
        o_ref[:, c0:c0 + LANES] = o.astype(o_ref.dtype)

        @pl.when(c == nc - 1)
        def _():
            st_ref[2 * pr] = s_new[0:HEAD_DIM, 0:HEAD_DIM]
            st_ref[2 * pr + 1] = pltpu.roll(s_new, HEAD_DIM, 1)[HEAD_DIM:2 * HEAD_DIM, 0:HEAD_DIM]


def _ret_prompt(ret, tabs, b, t):
    nc = t // RET_CHUNK
    rowblk = lambda bi, i: (bi * nc + i, 0)
    c3 = lambda bi, i: (0, 0, 0)
    dmat, rowdec, kdec, sdec, bmask = tabs
    return pl.pallas_call(
        _ret_prompt_kernel,
        grid=(b, nc),
        in_specs=[
            pl.BlockSpec((RET_CHUNK, 4 * RET_W), rowblk),
            pl.BlockSpec(dmat.shape, c3),
            pl.BlockSpec(rowdec.shape, c3),
            pl.BlockSpec(kdec.shape, c3),
            pl.BlockSpec(sdec.shape, c3),
            pl.BlockSpec(bmask.shape, lambda bi, i: (0, 0)),
        ],
        out_specs=(pl.BlockSpec((RET_CHUNK, RET_W), rowblk),
                   pl.BlockSpec((None, RET_HEADS, HEAD_DIM, HEAD_DIM), lambda bi, i: (bi, 0, 0, 0))),
        out_shape=(jax.ShapeDtypeStruct((b * t, RET_W), _MXU),
                   jax.ShapeDtypeStruct((b, RET_HEADS, HEAD_DIM, HEAD_DIM), _F32)),
        scratch_shapes=[pltpu.VMEM((RET_HEADS // 2, LANES, LANES), _F32)],
        compiler_params=_params(("parallel", "arbitrary")),
        name="ret_prompt",
    )(ret, dmat, rowdec, kdec, sdec, bmask)


def _sample_queries(q_ref):
    qf = q_ref[...].astype(_F32)
    heads = _group_queries(qf, 1)
    row = lax.broadcasted_iota(jnp.int32, (NSA_HEADS, LANES), 0)
    q8 = jnp.zeros((NSA_HEADS, LANES), _F32)
    for h in range(NSA_HEADS):
        q8 = jnp.where(row == h, jnp.broadcast_to(heads[h], (NSA_HEADS, LANES)), q8)
    return q8


def _s_topk_kernel(q_pos, nsel, q_ref, kcb_ref, vcb_ref, cover_ref, idx_ref, oc_ref):
    q8 = _sample_queries(q_ref)
    nb16 = kcb_ref.shape[0]
    s_c = _dot_nt(q8, kcb_ref[...])
    blk_end = lax.broadcasted_iota(jnp.int32, (NSA_HEADS, nb16), 1) * CMP_STRIDE + (CMP_BLK - 1)
    p_c = _masked_softmax(s_c, blk_end <= q_pos)
    oc_ref[...] = _dot(p_c, vcb_ref[...])

    npad = cover_ref.shape[0]
    row = lax.broadcasted_iota(jnp.int32, (NSA_HEADS, nb16), 0)
    psum = jnp.zeros((NSA_HEADS, nb16), _F32)
    for g in range(KV_GROUPS):
        acc = p_c[g * HPG:g * HPG + 1]
        for hh in range(1, HPG):
            acc = acc + p_c[g * HPG + hh:g * HPG + hh + 1]
        psum = jnp.where(row == g, jnp.broadcast_to(acc, (NSA_HEADS, nb16)), psum)
    imp = jnp.zeros((NSA_HEADS, npad), _F32)
    for part in _split3(psum):
        imp = imp + lax.dot_general(part, cover_ref[...], (((1,), (1,)), ((), ())),
                                    preferred_element_type=_F32)
    j = lax.broadcasted_iota(jnp.int32, (NSA_HEADS, npad), 1)
    cur = q_pos // SEL_BLK
    forced = (j == 0) | (j == cur) | (j == cur - 1)
    valid = j * SEL_BLK <= q_pos
    score = jnp.where(forced, jnp.inf, jnp.where(valid, imp, -jnp.inf))
    jf = j.astype(_F32)
    alive = jnp.where(j < nsel, 1.0, 0.0)
    lane = lax.broadcasted_iota(jnp.int32, (NSA_HEADS, LANES), 1)
    idx = jnp.zeros((NSA_HEADS, LANES), _F32)
    for r in range(min(SEL_TOPN, nsel)):
        live = alive > 0.0
        m = jnp.max(jnp.where(live, score, -jnp.inf), axis=-1, keepdims=True)
        cand = jnp.where(live, jnp.where(score == m, jf, float(npad)), float(npad))
        jmin = jnp.min(cand, axis=-1, keepdims=True)
        idx = jnp.where(lane == r, jmin, idx)
        alive = jnp.where(jf == jmin, 0.0, alive)
    idx_ref[...] = idx.astype(jnp.int32)


def _s_topk(q3, kcb, vcb, cover_s, q_pos, nsel):
    b2 = q3.shape[0]
    nb16 = kcb.shape[1]
    per_b = lambda i: (i, 0, 0)
    return pl.pallas_call(
        functools.partial(_s_topk_kernel, q_pos, nsel),
        grid=(b2,),
        in_specs=[
            pl.BlockSpec((None, 1, NSA_W), per_b),
            pl.BlockSpec((None, nb16, LANES), per_b),
            pl.BlockSpec((None, nb16, LANES), per_b),
            pl.BlockSpec(cover_s.shape, lambda i: (0, 0)),
        ],
        out_specs=(pl.BlockSpec((None, NSA_HEADS, LANES), per_b),
                   pl.BlockSpec((None, NSA_HEADS, LANES), per_b)),
        out_shape=(jax.ShapeDtypeStruct((b2, NSA_HEADS, LANES), jnp.int32),
                   jax.ShapeDtypeStruct((b2, NSA_HEADS, LANES), _F32)),
        compiler_params=_params(("parallel",)),
        name="sample_topk",
    )(q3, kcb, vcb, cover_s)


def _s_attn_kernel(q_pos, nsel, past, wbuf, idx_ref, pt_ref, *refs):
    del pt_ref
    n_top = min(SEL_TOPN, nsel)
    nblk = KV_GROUPS * n_top
    blk_refs = refs[:nblk]
    q_ref, gates_ref, oc_ref, rows_ref, win_ref, cw_ref, o_ref = refs[nblk:]
    b = pl.program_id(0)
    q8 = _sample_queries(q_ref)
    row1 = lax.broadcasted_iota(jnp.int32, (NSA_HEADS, 1), 0)
    is_g0 = row1 < HPG

    nk = n_top * SEL_BLK
    kblk = lax.broadcasted_iota(jnp.int32, (1, nk), 1) // SEL_BLK
    o_s = None
    for g in range(KV_GROUPS):
        kcat = jnp.concatenate([blk_refs[g * n_top + k][:, 0:LANES] for k in range(n_top)], axis=0)
        vcat = jnp.concatenate([blk_refs[g * n_top + k][:, LANES:2 * LANES] for k in range(n_top)], axis=0)
        s = _dot_nt(q8, kcat)
        bias = jnp.zeros((1, nk), _F32)
        has_new = jnp.zeros((1, 1), _F32)
        for k in range(n_top):
            jk = idx_ref[(b * KV_GROUPS + g) * n_top + k]
            is_new = jk == nsel - 1
            bias = jnp.where(kblk == k, jnp.where(is_new, NEG_BIG, 0.0), bias)
            has_new = jnp.where(is_new, 1.0, has_new)
        s = s + bias
        k_new = rows_ref[:, 2 * LANES:3 * LANES]
        v_new = rows_ref[:, 3 * LANES:4 * LANES]
        s_new = jnp.sum(q8 * k_new, axis=-1, keepdims=True) + jnp.where(has_new > 0.0, 0.0, NEG_BIG)
        m = jnp.maximum(jnp.max(s, axis=-1, keepdims=True), s_new)
        e = jnp.exp(s - m)
        e_new = jnp.exp(s_new - m)
        den = jnp.sum(e, axis=-1, keepdims=True) + e_new
        o_g = (_dot(e, vcat) + e_new * v_new) / den
        o_s = o_g if o_s is None else jnp.where(is_g0, o_s, o_g)

    cw = cw_ref[...]
    s_w = _dot_nt(q8, cw[:, 0:LANES])
    wpos = (past - wbuf) + lax.broadcasted_iota(jnp.int32, (1, wbuf), 1)
    wmask = (wpos <= q_pos) & (wpos > q_pos - WINDOW) & (wpos >= 0)
    s_w = jnp.where(wmask, s_w, NEG_BIG)
    s_wn = jnp.sum(q8 * win_ref[:, 0:LANES], axis=-1, keepdims=True)
    m = jnp.maximum(jnp.max(s_w, axis=-1, keepdims=True), s_wn)
    e = jnp.exp(s_w - m)
    e_new = jnp.exp(s_wn - m)
    den = jnp.sum(e, axis=-1, keepdims=True) + e_new
    o_w = (_dot(e, cw[:, LANES:2 * LANES]) + e_new * win_ref[:, LANES:2 * LANES]) / den

    gt = jnp.broadcast_to(gates_ref[...], (NSA_HEADS, LANES))
    lane = lax.broadcasted_iota(jnp.int32, (NSA_HEADS, LANES), 1)
    row = lax.broadcasted_iota(jnp.int32, (NSA_HEADS, LANES), 0)
    o = jnp.zeros((NSA_HEADS, LANES), _F32)
    for jb, ob in enumerate((oc_ref[...], o_s, o_w)):
        gcol = jnp.sum(jnp.where(lane == 3 * row + jb, gt, 0.0), axis=-1, keepdims=True)
        o = o + gcol * ob
    heads = [o[h:h + 1] for h in range(NSA_HEADS)]
    for k, s in enumerate(_place_heads(heads, 1)):
        o_ref[:, k * LANES:(k + 1) * LANES] = s.astype(o_ref.dtype)


def _s_attn(idx_flat, pt_flat, poolh, q3, gates3, oc, rows3, win3, cache_win3, q_pos, nsel, past, n_pages):
    b2 = q3.shape[0]
    wbuf = cache_win3.shape[1]
    n_top = min(SEL_TOPN, nsel)
    per_b = lambda i, idx, pt: (i, 0, 0)

    def blk_spec(g, k):
        def imap(i, idx, pt):
            j = jnp.minimum(idx[(i * KV_GROUPS + g) * n_top + k], nsel - 2)
            page = pt[i * n_pages + j // 2]
            return (page * 2 + j % 2, 0, 1)
        return pl.BlockSpec((None, SEL_BLK, 2 * LANES), imap)

    grid_spec = pltpu.PrefetchScalarGridSpec(
        num_scalar_prefetch=2,
        grid=(b2,),
        in_specs=[blk_spec(g, k) for g in range(KV_GROUPS) for k in range(n_top)] + [
            pl.BlockSpec((None, 1, NSA_W), per_b),
            pl.BlockSpec((None, 1, LANES), per_b),
            pl.BlockSpec((None, NSA_HEADS, LANES), per_b),
            pl.BlockSpec((None, 1, 4 * KV_W), per_b),
            pl.BlockSpec((None, 1, 2 * KV_W), per_b),
            pl.BlockSpec((None, wbuf, 2 * KV_W), per_b),
        ],
        out_specs=pl.BlockSpec((None, 1, NSA_W), per_b),
    )
    return pl.pallas_call(
        functools.partial(_s_attn_kernel, q_pos, nsel, past, wbuf),
        grid_spec=grid_spec,
        out_shape=jax.ShapeDtypeStruct((b2, 1, NSA_W), _F32),
        compiler_params=_params(("arbitrary",)),
        name="sample_attn",
    )(idx_flat, pt_flat, *([poolh] * (KV_GROUPS * n_top)), q3, gates3, oc, rows3, win3, cache_win3)


def _s_ret_kernel(q_ref, k_ref, v_ref, g_ref, gam_ref, st_ref, e1_ref, e2_ref, o_ref, sn_ref):
    q, k, v = q_ref[...], k_ref[...], v_ref[...]
    gam = gam_ref[...]
    st = st_ref[...]
    q_exp = _dot(q, e1_ref[...])
    k_exp = _dot(k, e1_ref[...])
    v_exp = _dot(v, e2_ref[...])
    prod = q_exp * st.astype(_MXU).astype(_F32)
    qs = jnp.zeros(q.shape, _F32)
    for part in _split3(prod):
        qs = qs + lax.dot_general(part, e2_ref[...], (((1,), (1,)), ((), ())), preferred_element_type=_F32)
    qr = q.astype(_MXU).astype(_F32)
    kr = k.astype(_MXU).astype(_F32)
    att = jnp.sum(qr * kr, axis=-1, keepdims=True)
    o = att.astype(_MXU).astype(_F32) * v.astype(_MXU).astype(_F32) + qs * gam
    sn_ref[...] = st * gam + k_exp * v_exp
    ms = jnp.mean(o * o, axis=-1, keepdims=True)
    o_ref[...] = (o * lax.rsqrt(ms + RMS_EPS) * jax.nn.silu(g_ref[...])).astype(o_ref.dtype)


def _s_ret(q, k, v, g, gam, st2, e1, e2, tr):
    n = q.shape[0]
    dd = HEAD_DIM * HEAD_DIM
    row = lambda i: (i, 0)
    const = lambda i: (0, 0)
    return pl.pallas_call(
        _s_ret_kernel,
        grid=(n // tr,),
        in_specs=[
            pl.BlockSpec((tr, HEAD_DIM), row),
            pl.BlockSpec((tr, HEAD_DIM), row),
            pl.BlockSpec((tr, HEAD_DIM), row),
            pl.BlockSpec((tr, HEAD_DIM), row),
            pl.BlockSpec((tr, 1), row),
            pl.BlockSpec((tr, dd), row),
            pl.BlockSpec((HEAD_DIM, dd), const),
            pl.BlockSpec((HEAD_DIM, dd), const),
        ],
        out_specs=(pl.BlockSpec((tr, HEAD_DIM), row), pl.BlockSpec((tr, dd), row)),
        out_shape=(jax.ShapeDtypeStruct((n, HEAD_DIM), _MXU), jax.ShapeDtypeStruct((n, dd), _F32)),
        compiler_params=_params(("parallel",)),
        name="sample_ret",
    )(q, k, v, g, gam, st2, e1, e2)


def _rope_tables(pos):
    half = HEAD_DIM // 2
    inv = 1.0 / (ROPE_THETA ** (jnp.arange(half, dtype=_F32) / half))
    ang = pos.astype(_F32)[:, None] * inv[None, :]
    cos, sin = jnp.cos(ang), jnp.sin(ang)
    zero = jnp.zeros_like(sin)
    reps = LANES // HEAD_DIM
    cos_t = jnp.tile(cos, (1, 2 * reps))
    sa = jnp.tile(jnp.concatenate([-sin, zero], axis=1), (1, reps))
    sb = jnp.tile(jnp.concatenate([zero, sin], axis=1), (1, reps))
    return cos_t, sa, sb


def _cover_matrix(nsel, nb16, nsel_pad):
    c0 = np.arange(nb16) * CMP_STRIDE
    s0 = np.arange(nsel_pad) * SEL_BLK
    m = (c0[None, :] < s0[:, None] + SEL_BLK) & (c0[None, :] + CMP_BLK > s0[:, None])
    m = m & (np.arange(nsel_pad)[:, None] < nsel)
    return jnp.asarray(m.astype(np.float32), dtype=_MXU)


def _retention_tables():
    lg = jnp.log(1.0 - 2.0 ** (-5.0 - jnp.arange(RET_HEADS, dtype=_F32)))
    c = RET_CHUNK
    i = jnp.arange(c, dtype=_F32)
    diff = i[:, None] - i[None, :]
    causal = diff >= 0
    dmat = jnp.where(causal[None], jnp.exp(jnp.where(causal, diff, 0.0)[None] * lg[:, None, None]), 0.0)
    lane_head = jnp.arange(LANES) // HEAD_DIM
    pair_lg = lg.reshape(RET_HEADS // 2, 2)[:, lane_head]
    rowdec = jnp.exp((i + 1.0)[None, :, None] * pair_lg[:, None, :])
    kdec = jnp.exp((c - 1.0 - i)[None, :, None] * pair_lg[:, None, :])
    sdec = jnp.broadcast_to(jnp.exp(c * pair_lg)[:, :, None], (RET_HEADS // 2, LANES, LANES))
    bmask = (lane_head[:, None] == lane_head[None, :]).astype(_F32)
    return lg, (dmat, rowdec, kdec, sdec, bmask)


def _compress_weights(pos, w1, w2):
    w1t = w1.reshape(CMP_BLK, HEAD_DIM, CMP_HID)
    z1 = jnp.zeros_like(w1t)
    w1bd = jnp.concatenate([jnp.concatenate([w1t, z1], axis=2), jnp.concatenate([z1, w1t], axis=2)], axis=1)
    z2 = jnp.zeros_like(w2)
    w2bd = jnp.concatenate([jnp.concatenate([w2, z2], axis=1), jnp.concatenate([z2, w2], axis=1)], axis=0)
    return jnp.tile(pos, (1, 2)), w1bd.astype(_MXU), w2bd.astype(_MXU)


def kernel(x_prompt, x_sample, cache_kv, cache_win, state_ret, page_table, ln1, w_in, cmp_pos_k, cmp_w1_k,
           cmp_w2_k, cmp_pos_v, cmp_w1_v, cmp_w2_v, w_out, ln2, w_gate, w_up, w_down, ln_f):
    depth = ln1.shape[0]
    assert depth == 1, "single-layer step"
    b, t, d = x_prompt.shape
    b2, s_s, _ = x_sample.shape
    assert s_s == 1
    n_pages = page_table.shape[1]
    past = n_pages * PAGE_SIZE
    wbuf = cache_win.shape[2]
    assert t % (2 * Q_BLK) == 0 and t >= WINDOW + Q_BLK and (t // SEL_BLK) % 8 == 0
    l = 0

    w = w_in[l]
    o_kv = NSA_W
    o_gt = o_kv + 6 * KV_W
    o_r = o_gt + NSA_HEADS * 3
    w_perm = jnp.concatenate([
        w[:, :o_gt], w[:, o_r:], w[:, o_gt:o_r],
        jnp.zeros((d, LANES - NSA_HEADS * 3), w.dtype)], axis=1).astype(_MXU)
    pos_k, w1k, w2k = _compress_weights(cmp_pos_k[l], cmp_w1_k[l], cmp_w2_k[l])
    pos_v, w1v, w2v = _compress_weights(cmp_pos_v[l], cmp_w1_v[l], cmp_w2_v[l])
    cpos = jnp.stack([pos_k, pos_v])
    cw1 = jnp.stack([w1k, w1v])
    cw2 = jnp.stack([w2k, w2v])
    wo, wg, wu, wd = (a[l].astype(_MXU) for a in (w_out, w_gate, w_up, w_down))
    ln1r, ln2r, lnfr = ln1[l][None, :], ln2[l][None, :], ln_f[None, :]
    lg, ret_tabs = _retention_tables()

    tm = 256
    xp2 = x_prompt.reshape(b * t, d)
    q, rows, win, kvb, gates, ret = _proj(xp2, ln1r, w_perm, *_rope_tables(jnp.arange(t)), tm)
    rows3 = rows.reshape(b, t, 4 * KV_W)
    kcb, vcb = _compress_prompt(rows3, cpos, cw1, cw2)
    nsel_p = t // SEL_BLK
    nb16_p = t // CMP_STRIDE
    eoh = (jnp.arange(t)[:, None] // SEL_BLK == jnp.arange(LANES)[None, :]).astype(_MXU)
    o_nsa = _nsa_prompt(q, gates, kcb, vcb, kvb.reshape(b, t, 4 * KV_W), eoh,
                        _cover_matrix(nsel_p, nb16_p, nsel_p), b, t)
    o_ret, st_p = _ret_prompt(ret, ret_tabs, b, t)
    y_prompt = _post(xp2, o_nsa, o_ret, wo, ln2r, wg, wu, wd, lnfr, tm).reshape(b, t, d)
    kv_prompt = rows.reshape(1, b, t, 4, KV_GROUPS, HEAD_DIM)
    win_keep = min(WINDOW, t)
    win_prompt = win.reshape(b, t, 2, KV_GROUPS, HEAD_DIM)[None, :, t - win_keep:]
    ret_prompt = st_p[None]

    xs2 = x_sample.reshape(b2, d)
    tms = min(tm, b2)
    pos_s = jnp.full((b2,), past, jnp.int32)
    q_s, rows_s, win_s, _, gates_s, ret_s = _proj(xs2, ln1r, w_perm, *_rope_tables(pos_s), tms)
    n_pool = cache_kv.shape[1]
    pool3 = cache_kv[l].reshape(n_pool, PAGE_SIZE, 4 * KV_W)
    pt_flat = page_table.reshape(-1).astype(jnp.int32)
    kcb_s, vcb_s = _compress_sample(pool3, pt_flat, b2, n_pages, cpos, cw1, cw2)
    seq_len = past + 1
    nsel_s = -(-seq_len // SEL_BLK)
    nsel_pad = -(-nsel_s // LANES) * LANES
    nb16_s = past // CMP_STRIDE
    q3 = q_s.astype(_F32).reshape(b2, 1, NSA_W)
    idx8, oc = _s_topk(q3, kcb_s, vcb_s, _cover_matrix(nsel_s, nb16_s, nsel_pad), past, nsel_s)
    n_top = min(SEL_TOPN, nsel_s)
    idx_flat = idx8[:, :KV_GROUPS, :n_top].reshape(-1)
    poolh = cache_kv[l].reshape(n_pool * 2, SEL_BLK, 4 * KV_W)
    cache_win3 = cache_win[l].reshape(b2, wbuf, 2 * KV_W)
    o_nsa_s = _s_attn(idx_flat, pt_flat, poolh, q3, gates_s.reshape(b2, 1, LANES), oc,
                      rows_s.reshape(b2, 1, 4 * KV_W), win_s.reshape(b2, 1, 2 * KV_W), cache_win3,
                      past, nsel_s, past, n_pages).reshape(b2, NSA_W).astype(_MXU)
    nr = b2 * RET_HEADS
    heads = lambda a: a.reshape(nr, HEAD_DIM)
    rq, rk, rv, rg = (heads(ret_s[:, k * RET_W:(k + 1) * RET_W]) for k in range(4))
    gam = jnp.tile(jnp.exp(lg), (b2,))[:, None]
    dd = HEAD_DIM * HEAD_DIM
    lane_d = jnp.arange(dd) // HEAD_DIM
    lane_e = jnp.arange(dd) % HEAD_DIM
    e1 = (jnp.arange(HEAD_DIM)[:, None] == lane_d[None, :]).astype(_MXU)
    e2 = (jnp.arange(HEAD_DIM)[:, None] == lane_e[None, :]).astype(_MXU)
    o_ret_s, st_s = _s_ret(rq, rk, rv, rg, gam, state_ret[l].reshape(nr, dd), e1, e2, min(128, nr))
    y_sample = _post(xs2, o_nsa_s, o_ret_s.reshape(b2, RET_W), wo, ln2r, wg, wu, wd, lnfr, tms).reshape(b2, 1, d)
    kv_sample = rows_s.reshape(1, b2, 1, 4, KV_GROUPS, HEAD_DIM)
    win_sample = jnp.concatenate([cache_win[l][:, 1:], win_s.reshape(b2, 1, 2, KV_GROUPS, HEAD_DIM)], axis=1)[None]
    ret_sample = st_s.reshape(1, b2, RET_HEADS, HEAD_DIM, HEAD_DIM)
    return (y_prompt, y_sample, kv_prompt, kv_sample, win_prompt, win_sample, ret_prompt, ret_sample)
```

```python
import functools

import numpy as np
import jax
import jax.numpy as jnp
from jax import lax
from jax.experimental import pallas as pl
from jax.experimental.pallas import tpu as pltpu

HEAD_DIM = 64
NSA_HEADS = 8
RET_HEADS = 8
KV_GROUPS = 2
HPG = NSA_HEADS // KV_GROUPS
CMP_BLK = 32
CMP_STRIDE = 16
CMP_HID = 4 * HEAD_DIM
SEL_BLK = 64
SEL_TOPN = 16
WINDOW = 512
Q_BLK = 128
RET_CHUNK = 128
PAGE_SIZE = 128
ROPE_THETA = 10000.0
RMS_EPS = 1e-6

LANES = 128
NSA_W = NSA_HEADS * HEAD_DIM
RET_W = RET_HEADS * HEAD_DIM
KV_W = KV_GROUPS * HEAD_DIM
NEG_BIG = -(2.0 ** 100)
VMEM_LIMIT = 56 * 1024 * 1024

_MXU = jnp.bfloat16
_F32 = jnp.float32


def _dot(a, b):
    return jnp.dot(a.astype(_MXU), b.astype(_MXU), preferred_element_type=_F32)


def _dot_nt(a, b):
    return lax.dot_general(a.astype(_MXU), b.astype(_MXU), (((1,), (1,)), ((), ())),
                           preferred_element_type=_F32)


def _dot_tn(a, b):
    return lax.dot_general(a.astype(_MXU), b.astype(_MXU), (((0,), (0,)), ((), ())),
                           preferred_element_type=_F32)


def _split3(x):
    hi = x.astype(_MXU)
    r1 = x - hi.astype(_F32)
    mid = r1.astype(_MXU)
    lo = (r1 - mid.astype(_F32)).astype(_MXU)
    return hi, mid, lo


def _params(sem):
    return pltpu.CompilerParams(dimension_semantics=sem, vmem_limit_bytes=VMEM_LIMIT)


_C_Q, _C_KV, _C_RQ, _C_RK, _C_RV, _C_RG, _C_GT = 0, 512, 1280, 1792, 2304, 2816, 3328
_PROJ_COLS = 3456


def _proj_kernel(x_ref, g_ref, w_ref, cos_ref, sa_ref, sb_ref,
                 q_ref, rows_ref, win_ref, rows_t_ref, win_t_ref, kk_ref, vt_ref, gates_ref, ret_ref):
    x = x_ref[...]
    ms = jnp.mean(x * x, axis=-1, keepdims=True)
    h = (x * lax.rsqrt(ms + RMS_EPS) * g_ref[...]).astype(_MXU)
    cos, sa, sb = cos_ref[...], sa_ref[...], sb_ref[...]

    def seg(c0, n):
        return jnp.dot(h, w_ref[:, c0:c0 + n], preferred_element_type=_F32)

    def rope(p):
        return p * cos + pltpu.roll(p, LANES - 32, 1) * sa + pltpu.roll(p, 32, 1) * sb

    def slab(p, s):
        return p[:, s * LANES:(s + 1) * LANES]

    scale = HEAD_DIM ** -0.5
    p = seg(_C_Q, NSA_W)
    for s in range(4):
        q_ref[:, s * LANES:(s + 1) * LANES] = (rope(slab(p, s)) * scale).astype(q_ref.dtype)
    p = seg(_C_KV, 6 * KV_W)
    kc, vc = rope(slab(p, 0)), slab(p, 1)
    ks, vs = rope(slab(p, 2)), slab(p, 3)
    kw, vw = rope(slab(p, 4)), slab(p, 5)
    rows_ref[:, 0:128] = kc
    rows_ref[:, 128:256] = vc
    rows_ref[:, 256:384] = ks
    rows_ref[:, 384:512] = vs
    win_ref[:, 0:128] = kw
    win_ref[:, 128:256] = vw
    kk_ref[:, 0:128] = ks.astype(kk_ref.dtype)
    kk_ref[:, 128:256] = kw.astype(kk_ref.dtype)
    vs_t, vw_t = vs.T, vw.T
    rows_t_ref[0] = kc.T
    rows_t_ref[1] = vc.T
    rows_t_ref[2] = ks.T
    rows_t_ref[3] = vs_t
    win_t_ref[0] = kw.T
    win_t_ref[1] = vw_t
    vt_ref[0] = vs_t.astype(vt_ref.dtype)
    vt_ref[1] = vw_t.astype(vt_ref.dtype)
    p = seg(_C_RQ, RET_W)
    for s in range(4):
        ret_ref[:, s * LANES:(s + 1) * LANES] = rope(slab(p, s))
    p = seg(_C_RK, RET_W)
    for s in range(4):
        ret_ref[:, RET_W + s * LANES:RET_W + (s + 1) * LANES] = rope(slab(p, s)) * scale
    ret_ref[:, 2 * RET_W:3 * RET_W] = seg(_C_RV, RET_W)
    ret_ref[:, 3 * RET_W:4 * RET_W] = seg(_C_RG, RET_W)
    gates_ref[...] = jax.nn.sigmoid(seg(_C_GT, LANES))


def _proj(x2, ln, w_perm, cos, sa, sb, tm):
    n, d = x2.shape
    tt = cos.shape[0]
    nt = tt // tm
    row = lambda i: (i, 0)
    tab = lambda i: (i % nt, 0)
    const = lambda i: (0, 0)
    out_shape = (
        jax.ShapeDtypeStruct((n, NSA_W), _MXU),
        jax.ShapeDtypeStruct((n, 4 * KV_W), _F32),
        jax.ShapeDtypeStruct((n, 2 * KV_W), _F32),
        jax.ShapeDtypeStruct((n // tt, 4, KV_W, tt), _F32),
        jax.ShapeDtypeStruct((n // tt, 2, KV_W, tt), _F32),
        jax.ShapeDtypeStruct((n, 2 * KV_W), _MXU),
        jax.ShapeDtypeStruct((n // tt, 2, KV_W, tt), _MXU),
        jax.ShapeDtypeStruct((n, LANES), _F32),
        jax.ShapeDtypeStruct((n, 4 * RET_W), _F32),
    )
    return pl.pallas_call(
        _proj_kernel,
        grid=(n // tm,),
        in_specs=[
            pl.BlockSpec((tm, d), row),
            pl.BlockSpec((1, d), const),
            pl.BlockSpec((d, _PROJ_COLS), const),
            pl.BlockSpec((tm, LANES), tab),
            pl.BlockSpec((tm, LANES), tab),
            pl.BlockSpec((tm, LANES), tab),
        ],
        out_specs=(
            pl.BlockSpec((tm, NSA_W), row),
            pl.BlockSpec((tm, 4 * KV_W), row),
            pl.BlockSpec((tm, 2 * KV_W), row),
            pl.BlockSpec((None, 4, KV_W, tm), lambda i: (i // nt, 0, 0, i % nt)),
            pl.BlockSpec((None, 2, KV_W, tm), lambda i: (i // nt, 0, 0, i % nt)),
            pl.BlockSpec((tm, 2 * KV_W), row),
            pl.BlockSpec((None, 2, KV_W, tm), lambda i: (i // nt, 0, 0, i % nt)),
            pl.BlockSpec((tm, LANES), row),
            pl.BlockSpec((tm, 4 * RET_W), row),
        ),
        out_shape=out_shape,
        compiler_params=_params(("parallel",)),
        name="proj",
    )(x2, ln, w_perm, cos, sa, sb)


def _post_kernel(x_ref, on_ref, or_ref, wo_ref, g2_ref, wg_ref, wu_ref, wd_ref, gf_ref, y_ref):
    x = x_ref[...]
    mix = (jnp.dot(on_ref[...], wo_ref[0:NSA_W, :], preferred_element_type=_F32)
           + jnp.dot(or_ref[...], wo_ref[NSA_W:NSA_W + RET_W, :], preferred_element_type=_F32))
    x1 = x + mix
    ms = jnp.mean(x1 * x1, axis=-1, keepdims=True)
    h = (x1 * lax.rsqrt(ms + RMS_EPS) * g2_ref[...]).astype(_MXU)
    a = jax.nn.silu(jnp.dot(h, wg_ref[...], preferred_element_type=_F32))
    a = a * jnp.dot(h, wu_ref[...], preferred_element_type=_F32)
    y = x1 + jnp.dot(a.astype(_MXU), wd_ref[...], preferred_element_type=_F32)
    ms = jnp.mean(y * y, axis=-1, keepdims=True)
    y_ref[...] = y * lax.rsqrt(ms + RMS_EPS) * gf_ref[...]


def _post(x2, o_nsa, o_ret, w_out, ln2, w_gate, w_up, w_down, ln_f, tm):
    n, d = x2.shape
    dff = w_gate.shape[1]
    row = lambda i: (i, 0)
    const = lambda i: (0, 0)
    once = dict(pipeline_mode=pl.Buffered(1))
    return pl.pallas_call(
        _post_kernel,
        grid=(n // tm,),
        in_specs=[
            pl.BlockSpec((tm, d), row),
            pl.BlockSpec((tm, NSA_W), row),
            pl.BlockSpec((tm, RET_W), row),
            pl.BlockSpec((d, d), const, **once),
            pl.BlockSpec((1, d), const),
            pl.BlockSpec((d, dff), const, **once),
            pl.BlockSpec((d, dff), const, **once),
            pl.BlockSpec((dff, d), const, **once),
            pl.BlockSpec((1, d), const),
        ],
        out_specs=pl.BlockSpec((tm, d), row),
        out_shape=jax.ShapeDtypeStruct((n, d), _F32),
        compiler_params=_params(("parallel",)),
        name="post",
    )(x2, o_nsa, o_ret, w_out, ln2, w_gate, w_up, w_down, ln_f)


def _compress_body(src_refs, nb16, pos_ref, w1_ref, w2_ref):
    outs = []
    half = CMP_STRIDE // 2
    for c in range(2):
        a = jnp.zeros((nb16, 2 * CMP_HID), _F32)
        b = jnp.zeros((nb16, 2 * CMP_HID), _F32)
        for tp in range(half):
            t0, t1 = 2 * tp, 2 * tp + 1
            x0 = src_refs[c][pl.ds(t0, nb16, stride=CMP_STRIDE), :]
            x1 = src_refs[c][pl.ds(t1, nb16, stride=CMP_STRIDE), :]
            xa = jnp.concatenate([x0 + pos_ref[c, t0:t0 + 1, :], x1 + pos_ref[c, t1:t1 + 1, :]], axis=1)
            xb = jnp.concatenate([x0 + pos_ref[c, CMP_STRIDE + t0:CMP_STRIDE + t0 + 1, :],
                                  x1 + pos_ref[c, CMP_STRIDE + t1:CMP_STRIDE + t1 + 1, :]], axis=1)
            a = a + _dot(xa, w1_ref[c, tp])
            b = b + _dot(xb, w1_ref[c, half + tp])
        hid = a + pltpu.roll(b, nb16 - 1, 0)
        outs.append(_dot(jax.nn.gelu(hid), w2_ref[c]))
    return outs


def _fill_token_rows(buf_ref, c, r0, plane):
    buf_ref[c, r0:r0 + LANES, :] = plane.T


def _compress_prompt_kernel(rt_ref, pos_ref, w1_ref, w2_ref, kcb_ref, vcbt_ref, buf_ref):
    nb16 = kcb_ref.shape[0]
    for j in range(nb16 * CMP_STRIDE // LANES):
        for c in range(2):
            _fill_token_rows(buf_ref, c, j * LANES, rt_ref[c, :, j * LANES:(j + 1) * LANES])
    kcb, vcb = _compress_body((buf_ref.at[0], buf_ref.at[1]), nb16, pos_ref, w1_ref, w2_ref)
    kcb_ref[...] = kcb
    vcbt_ref[...] = vcb.T


def _compress_prompt(rows_t, pos, w1bd, w2bd):
    b, _, _, t = rows_t.shape
    nb16 = t // CMP_STRIDE
    c3 = lambda i: (0, 0, 0)
    c4 = lambda i: (0, 0, 0, 0)
    out = jax.ShapeDtypeStruct((b, nb16, LANES), _F32)
    return pl.pallas_call(
        _compress_prompt_kernel,
        grid=(b,),
        in_specs=[
            pl.BlockSpec((None, 2, KV_W, t), lambda i: (i, 0, 0, 0)),
            pl.BlockSpec(pos.shape, c3),
            pl.BlockSpec(w1bd.shape, c4),
            pl.BlockSpec(w2bd.shape, c3),
        ],
        out_specs=(pl.BlockSpec((None, nb16, LANES), lambda i: (i, 0, 0)),
                   pl.BlockSpec((None, LANES, nb16), lambda i: (i, 0, 0))),
        out_shape=(out, jax.ShapeDtypeStruct((b, LANES, nb16), _F32)),
        scratch_shapes=[pltpu.VMEM((2, t, LANES), _F32)],
        compiler_params=_params(("parallel",)),
        name="compress_prompt",
    )(rows_t, pos, w1bd, w2bd)


def _compress_sample_kernel(n_pages, pt_ref, *refs):
    page_refs = refs[:n_pages]
    pos_ref, w1_ref, w2_ref, kcb_ref, vcb_ref, buf_ref = refs[n_pages:]
    for j in range(n_pages):
        for c in range(2):
            _fill_token_rows(buf_ref, c, j * PAGE_SIZE, page_refs[j][c])
    nb16 = kcb_ref.shape[0]
    kcb, vcb = _compress_body((buf_ref.at[0], buf_ref.at[1]), nb16, pos_ref, w1_ref, w2_ref)
    kcb_ref[...] = kcb
    vcb_ref[...] = vcb


def _compress_sample(pool_t, page_table_flat, b2, n_pages, pos, w1bd, w2bd):
    past = n_pages * PAGE_SIZE
    nb16 = past // CMP_STRIDE
    c3 = lambda i, pt: (0, 0, 0)
    c4 = lambda i, pt: (0, 0, 0, 0)

    def page_spec(j):
        return pl.BlockSpec((None, 2, KV_W, PAGE_SIZE), lambda i, pt: (pt[i * n_pages + j], 0, 0, 0))

    out = jax.ShapeDtypeStruct((b2, nb16, LANES), _F32)
    grid_spec = pltpu.PrefetchScalarGridSpec(
        num_scalar_prefetch=1,
        grid=(b2,),
        in_specs=[page_spec(j) for j in range(n_pages)] + [
            pl.BlockSpec(pos.shape, c3),
            pl.BlockSpec(w1bd.shape, c4),
            pl.BlockSpec(w2bd.shape, c3),
        ],
        out_specs=(pl.BlockSpec((None, nb16, LANES), lambda i, pt: (i, 0, 0)),
                   pl.BlockSpec((None, nb16, LANES), lambda i, pt: (i, 0, 0))),
        scratch_shapes=[pltpu.VMEM((2, past, LANES), _F32)],
    )
    return pl.pallas_call(
        functools.partial(_compress_sample_kernel, n_pages),
        grid_spec=grid_spec,
        out_shape=(out, out),
        compiler_params=_params(("arbitrary",)),
        name="compress_sample",
    )(page_table_flat, *([pool_t] * n_pages), pos, w1bd, w2bd)


def _group_queries(qf, rows):
    lane = lax.broadcasted_iota(jnp.int32, (rows, LANES), 1)
    out = []
    for h in range(NSA_HEADS):
        g = h // HPG
        s = qf[:, (h // 2) * LANES:(h // 2 + 1) * LANES]
        if (h % 2) != g:
            s = pltpu.roll(s, HEAD_DIM, 1)
        out.append(jnp.where((lane >= g * HEAD_DIM) & (lane < (g + 1) * HEAD_DIM), s, 0.0))
    return out


def _masked_softmax(s, mask):
    s = jnp.where(mask, s, -jnp.inf)
    m = jnp.max(s, axis=-1, keepdims=True)
    m = jnp.where(m > -jnp.inf, m, 0.0)
    e = jnp.where(mask, jnp.exp(s - m), 0.0)
    return e / jnp.maximum(jnp.sum(e, axis=-1, keepdims=True), 1e-30)


def _place_heads(o_heads, rows):
    lane = lax.broadcasted_iota(jnp.int32, (rows, LANES), 1)
    slabs = []
    for k in range(NSA_HEADS // 2):
        pair = []
        for h in (2 * k, 2 * k + 1):
            g = h // HPG
            o = o_heads[h]
            if (h % 2) != g:
                o = pltpu.roll(o, HEAD_DIM, 1)
            pair.append(o)
        slabs.append(jnp.where(lane < HEAD_DIM, pair[0], pair[1]))
    return slabs


def _softmax_keys(s):
    m = jnp.max(s, axis=0, keepdims=True)
    m = jnp.where(m > -jnp.inf, m, 0.0)
    e = jnp.exp(s - m)
    return e * (1.0 / jnp.maximum(jnp.sum(e, axis=0, keepdims=True), 1e-30))


def _nsa_prompt_kernel(nsel, q_ref, gates_ref, kcb_ref, vcbt_ref, kk_ref, vt_ref, eoh_ref, cover_ref,
                       o_ref, acc_ref):
    i = pl.program_id(1)
    tq = Q_BLK
    ncol = NSA_HEADS * tq
    qf = q_ref[...].astype(_F32)
    q2 = jnp.concatenate(_group_queries(qf, tq), axis=0).astype(_MXU)
    qpos_1 = i * tq + lax.broadcasted_iota(jnp.int32, (1, tq), 1)

    def per_head(x):
        return jnp.concatenate([x] * NSA_HEADS, axis=1)

    nb16 = kcb_ref.shape[0]
    blk_end = lax.broadcasted_iota(jnp.int32, (nb16, 1), 0) * CMP_STRIDE + (CMP_BLK - 1)
    bias_c = jnp.where(blk_end <= qpos_1, 0.0, -jnp.inf)
    p_c = _softmax_keys(_dot_nt(kcb_ref[...], q2) + per_head(bias_c))
    o_c = _dot(vcbt_ref[...], p_c)

    jidx = lax.broadcasted_iota(jnp.int32, (nsel, tq), 0)
    qpos_l = i * tq + lax.broadcasted_iota(jnp.int32, (nsel, tq), 1)
    cur = qpos_l // SEL_BLK
    forced = (jidx == 0) | (jidx == cur) | (jidx == cur - 1)
    valid = jidx * SEL_BLK <= qpos_l
    n_top = min(SEL_TOPN, nsel)
    sub = lax.broadcasted_iota(jnp.int32, (8, tq), 0)
    q_sel = []
    for g in range(KV_GROUPS):
        psum = p_c[:, (g * HPG) * tq:(g * HPG + 1) * tq]
        for hh in range(1, HPG):
            psum = psum + p_c[:, (g * HPG + hh) * tq:(g * HPG + hh + 1) * tq]
        imp_t = jnp.zeros((nsel, tq), _F32)
        for part in _split3(psum):
            imp_t = imp_t + jnp.dot(cover_ref[...], part, preferred_element_type=_F32)
        score = jnp.where(forced, jnp.inf, jnp.where(valid, imp_t, -jnp.inf))
        nv = nsel // 8
        sc_v = [score[8 * v:8 * v + 8] for v in range(nv)]
        rank_v = [jnp.zeros((8, tq), jnp.int32) for _ in range(nv)]
        for jp in range(nsel):
            rowb = jnp.broadcast_to(score[jp:jp + 1, :], (8, tq))
            for v in range(nv):
                if v > jp // 8:
                    beats = jnp.where(rowb >= sc_v[v], 1, 0)
                elif v < jp // 8:
                    beats = jnp.where(rowb > sc_v[v], 1, 0)
                else:
                    beats = jnp.where(sub > (jp % 8), jnp.where(rowb >= sc_v[v], 1, 0),
                                      jnp.where(rowb > sc_v[v], 1, 0))
                rank_v[v] = rank_v[v] + beats
        sel_t = jnp.concatenate([jnp.where(r < n_top, 1.0, 0.0) for r in rank_v], axis=0)
        if nsel < LANES:
            sel_t = jnp.concatenate([sel_t, jnp.ones((LANES - nsel, tq), _F32)], axis=0)
        sel = sel_t.T
        bias = ((sel - 1.0) * (-NEG_BIG)).astype(_MXU)
        q_sel.extend([bias] * HPG)
    q_sel = jnp.concatenate(q_sel, axis=0)
    q_aug = jnp.concatenate([q2, q_sel], axis=1)

    tk = 2 * tq
    acc_ref[...] = jnp.zeros(acc_ref.shape, _F32)

    def key_tile(kt, m_old, l_old, causal):
        k0 = pl.multiple_of(kt * tk, tk)
        k_aug = jnp.concatenate([kk_ref[pl.ds(k0, tk), 0:LANES], eoh_ref[pl.ds(k0, tk), :]], axis=1)
        s = _dot_nt(k_aug, q_aug)
        if causal:
            kpos = k0 + lax.broadcasted_iota(jnp.int32, (tk, 1), 0)
            s = s + per_head(jnp.where(kpos <= qpos_1, 0.0, NEG_BIG))
        m_new = jnp.maximum(m_old, jnp.max(s, axis=0, keepdims=True))
        alpha = jnp.exp(m_old - m_new)
        p = jnp.exp(s - m_new)
        l_new = alpha * l_old + jnp.sum(p, axis=0, keepdims=True)
        acc_ref[...] = alpha * acc_ref[...] + _dot(vt_ref[0, :, pl.ds(k0, tk)], p)
        return m_new, l_new

    nt = (i + 2) // 2
    init = (jnp.full((1, ncol), 2.0 * NEG_BIG, _F32), jnp.zeros((1, ncol), _F32))
    m_run, l_run = lax.fori_loop(0, nt - 1, lambda kt, c: key_tile(kt, c[0], c[1], False), init)
    _, l_run = key_tile(nt - 1, m_run, l_run, True)
    o_s = acc_ref[...] * (1.0 / l_run)

    wk = WINDOW + tq
    start = pl.multiple_of(jnp.maximum(i * tq - WINDOW, 0), tq)
    wpos = start + lax.broadcasted_iota(jnp.int32, (wk, 1), 0)
    bias_w = jnp.where(wpos <= qpos_1, jnp.where(wpos > qpos_1 - WINDOW, 0.0, -jnp.inf), -jnp.inf)
    p_w = _softmax_keys(_dot_nt(kk_ref[pl.ds(start, wk), LANES:2 * LANES], q2) + per_head(bias_w))
    o_w = _dot(vt_ref[1, :, pl.ds(start, wk)], p_w)

    gt_t = gates_ref[...].T
    o_rows = []
    for h in range(NSA_HEADS):
        g = h // HPG
        rs = slice(g * HEAD_DIM, (g + 1) * HEAD_DIM)
        cs = slice(h * tq, (h + 1) * tq)
        o_rows.append(gt_t[3 * h:3 * h + 1] * o_c[rs, cs] + gt_t[3 * h + 1:3 * h + 2] * o_s[rs, cs]
                      + gt_t[3 * h + 2:3 * h + 3] * o_w[rs, cs])
    o_ref[...] = jnp.concatenate(o_rows, axis=0).T.astype(o_ref.dtype)


def _nsa_prompt(q, gates, kcb, vcbt, kk3, vt4, eoh, cover, b, t):
    nq = t // Q_BLK
    nb16 = t // CMP_STRIDE
    nsel = t // SEL_BLK
    rowblk = lambda bi, i: (bi * nq + i, 0)
    per_b = lambda bi, i: (bi, 0, 0)
    const = lambda bi, i: (0, 0)
    return pl.pallas_call(
        functools.partial(_nsa_prompt_kernel, nsel),
        grid=(b, nq),
        in_specs=[
            pl.BlockSpec((Q_BLK, NSA_W), rowblk),
            pl.BlockSpec((Q_BLK, LANES), rowblk),
            pl.BlockSpec((None, nb16, LANES), per_b),
            pl.BlockSpec((None, LANES, nb16), per_b),
            pl.BlockSpec((None, t, 2 * KV_W), per_b),
            pl.BlockSpec((None, 2, KV_W, t), lambda bi, i: (bi, 0, 0, 0)),
            pl.BlockSpec((t, LANES), const),
            pl.BlockSpec((nsel, nb16), const),
        ],
        out_specs=pl.BlockSpec((Q_BLK, NSA_W), rowblk),
        out_shape=jax.ShapeDtypeStruct((b * t, NSA_W), _MXU),
        scratch_shapes=[pltpu.VMEM((KV_W, NSA_HEADS * Q_BLK), _F32)],
        compiler_params=_params(("parallel", "arbitrary")),
        name="nsa_prompt",
    )(q, gates, kcb, vcbt, kk3, vt4, eoh, cover)


def _ret_prompt_kernel(ret_ref, dmat_ref, rowdec_ref, kdec_ref, sdec_ref, bmask_ref,
                       o_ref, st_ref, s_ref):
    c = pl.program_id(1)
    nc = pl.num_programs(1)
    tq = RET_CHUNK

    @pl.when(c == 0)
    def _():
        s_ref[...] = jnp.zeros(s_ref.shape, _F32)

    lane = lax.broadcasted_iota(jnp.int32, (tq, LANES), 1)
    lo = lane < HEAD_DIM
    bmask = bmask_ref[...]
    for pr in range(RET_HEADS // 2):
        c0 = pr * LANES
        q = ret_ref[:, c0:c0 + LANES]
        k = ret_ref[:, RET_W + c0:RET_W + c0 + LANES]
        v = ret_ref[:, 2 * RET_W + c0:2 * RET_W + c0 + LANES]
        g = ret_ref[:, 3 * RET_W + c0:3 * RET_W + c0 + LANES]
        q2 = jnp.concatenate([jnp.where(lo, q, 0.0), jnp.where(lo, 0.0, q)], axis=0)
        att = _dot_nt(q2, k) * jnp.concatenate([dmat_ref[2 * pr], dmat_ref[2 * pr + 1]], axis=0)
        o2 = _dot(att, v)
        o = jnp.where(lo, o2[0:tq], o2[tq:2 * tq])
        s_old = s_ref[pr]
        o = o + _dot(q, s_old) * rowdec_ref[pr]
        s_new = s_old * sdec_ref[pr] + _dot_tn(k * kdec_ref[pr], v) * bmask
        s_ref[pr] = s_new
        o_sq = o * o
        s0 = jnp.sum(jnp.where(lo, o_sq, 0.0), axis=-1, keepdims=True)
        s1 = jnp.sum(jnp.where(lo, 0.0, o_sq), axis=-1, keepdims=True)
        ms = jnp.where(lo, s0, s1) * (1.0 / HEAD_DIM)
        o = o * lax.rsqrt(ms + RMS_EPS) * jax.nn.silu(g)
        o_ref[:, c0:c0 + LANES] = o.astype(o_ref.dtype)

        @pl.when(c == nc - 1)
        def _():
            st_ref[2 * pr] = s_new[0:HEAD_DIM, 0:HEAD_DIM]
            st_ref[2 * pr + 1] = pltpu.roll(s_new, HEAD_DIM, 1)[HEAD_DIM:2 * HEAD_DIM, 0:HEAD_DIM]


def _ret_prompt(ret, tabs, b, t):
    nc = t // RET_CHUNK
    rowblk = lambda bi, i: (bi * nc + i, 0)
    c3 = lambda bi, i: (0, 0, 0)
    dmat, rowdec, kdec, sdec, bmask = tabs
    return pl.pallas_call(
        _ret_prompt_kernel,
        grid=(b, nc),
        in_specs=[
            pl.BlockSpec((RET_CHUNK, 4 * RET_W), rowblk),
            pl.BlockSpec(dmat.shape, c3),
            pl.BlockSpec(rowdec.shape, c3),
            pl.BlockSpec(kdec.shape, c3),
            pl.BlockSpec(sdec.shape, c3),
            pl.BlockSpec(bmask.shape, lambda bi, i: (0, 0)),
        ],
        out_specs=(pl.BlockSpec((RET_CHUNK, RET_W), rowblk),
                   pl.BlockSpec((None, RET_HEADS, HEAD_DIM, HEAD_DIM), lambda bi, i: (bi, 0, 0, 0))),
        out_shape=(jax.ShapeDtypeStruct((b * t, RET_W), _MXU),
                   jax.ShapeDtypeStruct((b, RET_HEADS, HEAD_DIM, HEAD_DIM), _F32)),
        scratch_shapes=[pltpu.VMEM((RET_HEADS // 2, LANES, LANES), _F32)],
        compiler_params=_params(("parallel", "arbitrary")),
        name="ret_prompt",
    )(ret, dmat, rowdec, kdec, sdec, bmask)


def _sample_queries(q_ref):
    qf = q_ref[...].astype(_F32)
    heads = _group_queries(qf, 1)
    row = lax.broadcasted_iota(jnp.int32, (NSA_HEADS, LANES), 0)
    q8 = jnp.zeros((NSA_HEADS, LANES), _F32)
    for h in range(NSA_HEADS):
        q8 = jnp.where(row == h, jnp.broadcast_to(heads[h], (NSA_HEADS, LANES)), q8)
    return q8


def _s_topk_kernel(q_pos, nsel, q_ref, kcb_ref, vcb_ref, cover_ref, idx_ref, oc_ref):
    q8 = _sample_queries(q_ref)
    nb16 = kcb_ref.shape[0]
    s_c = _dot_nt(q8, kcb_ref[...])
    blk_end = lax.broadcasted_iota(jnp.int32, (NSA_HEADS, nb16), 1) * CMP_STRIDE + (CMP_BLK - 1)
    p_c = _masked_softmax(s_c, blk_end <= q_pos)
    oc_ref[...] = _dot(p_c, vcb_ref[...])

    npad = cover_ref.shape[0]
    row = lax.broadcasted_iota(jnp.int32, (NSA_HEADS, nb16), 0)
    psum = jnp.zeros((NSA_HEADS, nb16), _F32)
    for g in range(KV_GROUPS):
        acc = p_c[g * HPG:g * HPG + 1]
        for hh in range(1, HPG):
            acc = acc + p_c[g * HPG + hh:g * HPG + hh + 1]
        psum = jnp.where(row == g, jnp.broadcast_to(acc, (NSA_HEADS, nb16)), psum)
    imp = jnp.zeros((NSA_HEADS, npad), _F32)
    for part in _split3(psum):
        imp = imp + lax.dot_general(part, cover_ref[...], (((1,), (1,)), ((), ())),
                                    preferred_element_type=_F32)
    j = lax.broadcasted_iota(jnp.int32, (NSA_HEADS, npad), 1)
    cur = q_pos // SEL_BLK
    forced = (j == 0) | (j == cur) | (j == cur - 1)
    valid = j * SEL_BLK <= q_pos
    score = jnp.where(forced, jnp.inf, jnp.where(valid, imp, -jnp.inf))
    jf = j.astype(_F32)
    alive = jnp.where(j < nsel, 1.0, 0.0)
    lane = lax.broadcasted_iota(jnp.int32, (NSA_HEADS, LANES), 1)
    idx = jnp.zeros((NSA_HEADS, LANES), _F32)
    for r in range(min(SEL_TOPN, nsel)):
        live = alive > 0.0
        m = jnp.max(jnp.where(live, score, -jnp.inf), axis=-1, keepdims=True)
        cand = jnp.where(live, jnp.where(score == m, jf, float(npad)), float(npad))
        jmin = jnp.min(cand, axis=-1, keepdims=True)
        idx = jnp.where(lane == r, jmin, idx)
        alive = jnp.where(jf == jmin, 0.0, alive)
    idx_ref[...] = idx.astype(jnp.int32)


def _s_topk(q3, kcb, vcb, cover_s, q_pos, nsel):
    b2 = q3.shape[0]
    nb16 = kcb.shape[1]
    per_b = lambda i: (i, 0, 0)
    return pl.pallas_call(
        functools.partial(_s_topk_kernel, q_pos, nsel),
        grid=(b2,),
        in_specs=[
            pl.BlockSpec((None, 1, NSA_W), per_b),
            pl.BlockSpec((None, nb16, LANES), per_b),
            pl.BlockSpec((None, nb16, LANES), per_b),
            pl.BlockSpec(cover_s.shape, lambda i: (0, 0)),
        ],
        out_specs=(pl.BlockSpec((None, NSA_HEADS, LANES), per_b),
                   pl.BlockSpec((None, NSA_HEADS, LANES), per_b)),
        out_shape=(jax.ShapeDtypeStruct((b2, NSA_HEADS, LANES), jnp.int32),
                   jax.ShapeDtypeStruct((b2, NSA_HEADS, LANES), _F32)),
        compiler_params=_params(("parallel",)),
        name="sample_topk",
    )(q3, kcb, vcb, cover_s)


def _s_attn_kernel(q_pos, nsel, past, wbuf, idx_ref, pt_ref, *refs):
    del pt_ref
    n_top = min(SEL_TOPN, nsel)
    nblk = KV_GROUPS * n_top
    blk_refs = refs[:nblk]
    q_ref, gates_ref, oc_ref, rows_ref, win_ref, cw_ref, o_ref = refs[nblk:]
    b = pl.program_id(0)
    q8 = _sample_queries(q_ref)
    row1 = lax.broadcasted_iota(jnp.int32, (NSA_HEADS, 1), 0)
    is_g0 = row1 < HPG

    nk = n_top * PAGE_SIZE
    lane_k = lax.broadcasted_iota(jnp.int32, (1, nk), 1)
    kslot = lane_k // PAGE_SIZE
    khalf = (lane_k % PAGE_SIZE) // SEL_BLK
    o_s = None
    for g in range(KV_GROUPS):
        kcat = jnp.concatenate([blk_refs[g * n_top + k][0] for k in range(n_top)], axis=1)
        vcat = jnp.concatenate([blk_refs[g * n_top + k][1] for k in range(n_top)], axis=1)
        s = _dot(q8, kcat)
        bias = jnp.full((1, nk), NEG_BIG, _F32)
        has_new = jnp.zeros((1, 1), _F32)
        for k in range(n_top):
            jk = idx_ref[(b * KV_GROUPS + g) * n_top + k]
            is_new = jk == nsel - 1
            half = jnp.where(is_new, -1, jk % 2)
            bias = jnp.where((kslot == k) & (khalf == half), 0.0, bias)
            has_new = jnp.where(is_new, 1.0, has_new)
        s = s + bias
        k_new = rows_ref[:, 2 * LANES:3 * LANES]
        v_new = rows_ref[:, 3 * LANES:4 * LANES]
        s_new = jnp.sum(q8 * k_new, axis=-1, keepdims=True) + jnp.where(has_new > 0.0, 0.0, NEG_BIG)
        m = jnp.maximum(jnp.max(s, axis=-1, keepdims=True), s_new)
        e = jnp.exp(s - m)
        e_new = jnp.exp(s_new - m)
        den = jnp.sum(e, axis=-1, keepdims=True) + e_new
        o_g = (_dot_nt(e, vcat) + e_new * v_new) / den
        o_s = o_g if o_s is None else jnp.where(is_g0, o_s, o_g)

    s_w = _dot(q8, cw_ref[0])
    wpos = (past - wbuf) + lax.broadcasted_iota(jnp.int32, (1, wbuf), 1)
    wmask = (wpos <= q_pos) & (wpos > q_pos - WINDOW) & (wpos >= 0)
    s_w = jnp.where(wmask, s_w, NEG_BIG)
    s_wn = jnp.sum(q8 * win_ref[:, 0:LANES], axis=-1, keepdims=True)
    m = jnp.maximum(jnp.max(s_w, axis=-1, keepdims=True), s_wn)
    e = jnp.exp(s_w - m)
    e_new = jnp.exp(s_wn - m)
    den = jnp.sum(e, axis=-1, keepdims=True) + e_new
    o_w = (_dot_nt(e, cw_ref[1]) + e_new * win_ref[:, LANES:2 * LANES]) / den

    gt = jnp.broadcast_to(gates_ref[...], (NSA_HEADS, LANES))
    lane = lax.broadcasted_iota(jnp.int32, (NSA_HEADS, LANES), 1)
    row = lax.broadcasted_iota(jnp.int32, (NSA_HEADS, LANES), 0)
    o = jnp.zeros((NSA_HEADS, LANES), _F32)
    for jb, ob in enumerate((oc_ref[...], o_s, o_w)):
        gcol = jnp.sum(jnp.where(lane == 3 * row + jb, gt, 0.0), axis=-1, keepdims=True)
        o = o + gcol * ob
    heads = [o[h:h + 1] for h in range(NSA_HEADS)]
    for k, s in enumerate(_place_heads(heads, 1)):
        o_ref[:, k * LANES:(k + 1) * LANES] = s.astype(o_ref.dtype)


def _s_attn(idx_flat, pt_flat, pool_t, q3, gates3, oc, rows3, win3, cache_win_t, q_pos, nsel, past, n_pages):
    b2 = q3.shape[0]
    wbuf = cache_win_t.shape[3]
    n_top = min(SEL_TOPN, nsel)
    per_b = lambda i, idx, pt: (i, 0, 0)

    def blk_spec(g, k):
        def imap(i, idx, pt):
            j = jnp.minimum(idx[(i * KV_GROUPS + g) * n_top + k], nsel - 2)
            return (pt[i * n_pages + j // 2], 1, 0, 0)
        return pl.BlockSpec((None, 2, KV_W, PAGE_SIZE), imap)

    grid_spec = pltpu.PrefetchScalarGridSpec(
        num_scalar_prefetch=2,
        grid=(b2,),
        in_specs=[blk_spec(g, k) for g in range(KV_GROUPS) for k in range(n_top)] + [
            pl.BlockSpec((None, 1, NSA_W), per_b),
            pl.BlockSpec((None, 1, LANES), per_b),
            pl.BlockSpec((None, NSA_HEADS, LANES), per_b),
            pl.BlockSpec((None, 1, 4 * KV_W), per_b),
            pl.BlockSpec((None, 1, 2 * KV_W), per_b),
            pl.BlockSpec((None, 2, KV_W, wbuf), lambda i, idx, pt: (i, 0, 0, 0)),
        ],
        out_specs=pl.BlockSpec((None, 1, NSA_W), per_b),
    )
    return pl.pallas_call(
        functools.partial(_s_attn_kernel, q_pos, nsel, past, wbuf),
        grid_spec=grid_spec,
        out_shape=jax.ShapeDtypeStruct((b2, 1, NSA_W), _F32),
        compiler_params=_params(("arbitrary",)),
        name="sample_attn",
    )(idx_flat, pt_flat, *([pool_t] * (KV_GROUPS * n_top)), q3, gates3, oc, rows3, win3, cache_win_t)


def _s_ret_kernel(q_ref, k_ref, v_ref, g_ref, gam_ref, st_ref, e1_ref, e2_ref, o_ref, sn_ref):
    q, k, v = q_ref[...], k_ref[...], v_ref[...]
    gam = gam_ref[...]
    st = st_ref[...]
    q_exp = _dot(q, e1_ref[...])
    k_exp = _dot(k, e1_ref[...])
    v_exp = _dot(v, e2_ref[...])
    prod = q_exp * st.astype(_MXU).astype(_F32)
    qs = jnp.zeros(q.shape, _F32)
    for part in _split3(prod):
        qs = qs + lax.dot_general(part, e2_ref[...], (((1,), (1,)), ((), ())), preferred_element_type=_F32)
    qr = q.astype(_MXU).astype(_F32)
    kr = k.astype(_MXU).astype(_F32)
    att = jnp.sum(qr * kr, axis=-1, keepdims=True)
    o = att.astype(_MXU).astype(_F32) * v.astype(_MXU).astype(_F32) + qs * gam
    sn_ref[...] = st * gam + k_exp * v_exp
    ms = jnp.mean(o * o, axis=-1, keepdims=True)
    o_ref[...] = (o * lax.rsqrt(ms + RMS_EPS) * jax.nn.silu(g_ref[...])).astype(o_ref.dtype)


def _s_ret(q, k, v, g, gam, st2, e1, e2, tr):
    n = q.shape[0]
    dd = HEAD_DIM * HEAD_DIM
    row = lambda i: (i, 0)
    const = lambda i: (0, 0)
    return pl.pallas_call(
        _s_ret_kernel,
        grid=(n // tr,),
        in_specs=[
            pl.BlockSpec((tr, HEAD_DIM), row),
            pl.BlockSpec((tr, HEAD_DIM), row),
            pl.BlockSpec((tr, HEAD_DIM), row),
            pl.BlockSpec((tr, HEAD_DIM), row),
            pl.BlockSpec((tr, 1), row),
            pl.BlockSpec((tr, dd), row),
            pl.BlockSpec((HEAD_DIM, dd), const),
            pl.BlockSpec((HEAD_DIM, dd), const),
        ],
        out_specs=(pl.BlockSpec((tr, HEAD_DIM), row), pl.BlockSpec((tr, dd), row)),
        out_shape=(jax.ShapeDtypeStruct((n, HEAD_DIM), _MXU), jax.ShapeDtypeStruct((n, dd), _F32)),
        compiler_params=_params(("parallel",)),
        name="sample_ret",
    )(q, k, v, g, gam, st2, e1, e2)


def _rope_tables(pos):
    half = HEAD_DIM // 2
    inv = 1.0 / (ROPE_THETA ** (jnp.arange(half, dtype=_F32) / half))
    ang = pos.astype(_F32)[:, None] * inv[None, :]
    cos, sin = jnp.cos(ang), jnp.sin(ang)
    zero = jnp.zeros_like(sin)
    reps = LANES // HEAD_DIM
    cos_t = jnp.tile(cos, (1, 2 * reps))
    sa = jnp.tile(jnp.concatenate([-sin, zero], axis=1), (1, reps))
    sb = jnp.tile(jnp.concatenate([zero, sin], axis=1), (1, reps))
    return cos_t, sa, sb


def _cover_matrix(nsel, nb16, nsel_pad):
    c0 = np.arange(nb16) * CMP_STRIDE
    s0 = np.arange(nsel_pad) * SEL_BLK
    m = (c0[None, :] < s0[:, None] + SEL_BLK) & (c0[None, :] + CMP_BLK > s0[:, None])
    m = m & (np.arange(nsel_pad)[:, None] < nsel)
    return jnp.asarray(m.astype(np.float32), dtype=_MXU)


def _retention_tables():
    lg = jnp.log(1.0 - 2.0 ** (-5.0 - jnp.arange(RET_HEADS, dtype=_F32)))
    c = RET_CHUNK
    i = jnp.arange(c, dtype=_F32)
    diff = i[:, None] - i[None, :]
    causal = diff >= 0
    dmat = jnp.where(causal[None], jnp.exp(jnp.where(causal, diff, 0.0)[None] * lg[:, None, None]), 0.0)
    lane_head = jnp.arange(LANES) // HEAD_DIM
    pair_lg = lg.reshape(RET_HEADS // 2, 2)[:, lane_head]
    rowdec = jnp.exp((i + 1.0)[None, :, None] * pair_lg[:, None, :])
    kdec = jnp.exp((c - 1.0 - i)[None, :, None] * pair_lg[:, None, :])
    sdec = jnp.broadcast_to(jnp.exp(c * pair_lg)[:, :, None], (RET_HEADS // 2, LANES, LANES))
    bmask = (lane_head[:, None] == lane_head[None, :]).astype(_F32)
    return lg, (dmat, rowdec, kdec, sdec, bmask)


def _compress_weights(pos, w1, w2):
    w1t = w1.reshape(CMP_BLK, HEAD_DIM, CMP_HID)
    z1 = jnp.zeros_like(w1t)
    w1bd = jnp.concatenate([jnp.concatenate([w1t, z1], axis=2), jnp.concatenate([z1, w1t], axis=2)], axis=1)
    z2 = jnp.zeros_like(w2)
    w2bd = jnp.concatenate([jnp.concatenate([w2, z2], axis=1), jnp.concatenate([z2, w2], axis=1)], axis=0)
    w1pair = w1bd.reshape(CMP_BLK // 2, 2 * LANES, 2 * CMP_HID)
    return jnp.tile(pos, (1, 2)), w1pair.astype(_MXU), w2bd.astype(_MXU)


def kernel(x_prompt, x_sample, cache_kv, cache_win, state_ret, page_table, ln1, w_in, cmp_pos_k, cmp_w1_k,
           cmp_w2_k, cmp_pos_v, cmp_w1_v, cmp_w2_v, w_out, ln2, w_gate, w_up, w_down, ln_f):
    depth = ln1.shape[0]
    assert depth == 1, "single-layer step"
    b, t, d = x_prompt.shape
    b2, s_s, _ = x_sample.shape
    assert s_s == 1
    n_pages = page_table.shape[1]
    past = n_pages * PAGE_SIZE
    wbuf = cache_win.shape[2]
    assert t % (2 * Q_BLK) == 0 and t >= WINDOW + Q_BLK and (t // SEL_BLK) % 8 == 0
    l = 0

    w = w_in[l]
    o_kv = NSA_W
    o_gt = o_kv + 6 * KV_W
    o_r = o_gt + NSA_HEADS * 3
    w_perm = jnp.concatenate([
        w[:, :o_gt], w[:, o_r:], w[:, o_gt:o_r],
        jnp.zeros((d, LANES - NSA_HEADS * 3), w.dtype)], axis=1).astype(_MXU)
    pos_k, w1k, w2k = _compress_weights(cmp_pos_k[l], cmp_w1_k[l], cmp_w2_k[l])
    pos_v, w1v, w2v = _compress_weights(cmp_pos_v[l], cmp_w1_v[l], cmp_w2_v[l])
    cpos = jnp.stack([pos_k, pos_v])
    cw1 = jnp.stack([w1k, w1v])
    cw2 = jnp.stack([w2k, w2v])
    wo, wg, wu, wd = (a[l].astype(_MXU) for a in (w_out, w_gate, w_up, w_down))
    ln1r, ln2r, lnfr = ln1[l][None, :], ln2[l][None, :], ln_f[None, :]
    lg, ret_tabs = _retention_tables()

    tm = 256
    xp2 = x_prompt.reshape(b * t, d)
    q, _, _, rows_t, win_t, kk, vt, gates, ret = _proj(xp2, ln1r, w_perm, *_rope_tables(jnp.arange(t)), tm)
    kcb, vcbt = _compress_prompt(rows_t, cpos, cw1, cw2)
    nsel_p = t // SEL_BLK
    nb16_p = t // CMP_STRIDE
    eoh = (jnp.arange(t)[:, None] // SEL_BLK == jnp.arange(LANES)[None, :]).astype(_MXU)
    o_nsa = _nsa_prompt(q, gates, kcb, vcbt, kk.reshape(b, t, 2 * KV_W), vt, eoh,
                        _cover_matrix(nsel_p, nb16_p, nsel_p), b, t)
    o_ret, st_p = _ret_prompt(ret, ret_tabs, b, t)
    y_prompt = _post(xp2, o_nsa, o_ret, wo, ln2r, wg, wu, wd, lnfr, tm).reshape(b, t, d)
    to_cache = lambda a: jnp.transpose(a.reshape(a.shape[0], a.shape[1], KV_GROUPS, HEAD_DIM, a.shape[3]),
                                       (0, 4, 1, 2, 3))[None]
    kv_prompt = to_cache(rows_t)
    win_keep = min(WINDOW, t)
    win_prompt = to_cache(win_t[:, :, :, t - win_keep:])
    ret_prompt = st_p[None]

    xs2 = x_sample.reshape(b2, d)
    tms = min(tm, b2)
    pos_s = jnp.full((b2,), past, jnp.int32)
    q_s, rows_s, win_s, rows_st, _, _, _, gates_s, ret_s = _proj(xs2, ln1r, w_perm, *_rope_tables(pos_s), tms)
    n_pool = cache_kv.shape[1]
    pool_t = jnp.transpose(cache_kv[l], (0, 2, 3, 4, 1)).reshape(n_pool, 4, KV_W, PAGE_SIZE)
    cache_win_t = jnp.transpose(cache_win[l], (0, 2, 3, 4, 1)).reshape(b2, 2, KV_W, wbuf)
    pt_flat = page_table.reshape(-1).astype(jnp.int32)
    kcb_s, vcb_s = _compress_sample(pool_t, pt_flat, b2, n_pages, cpos, cw1, cw2)
    seq_len = past + 1
    nsel_s = -(-seq_len // SEL_BLK)
    nsel_pad = -(-nsel_s // LANES) * LANES
    nb16_s = past // CMP_STRIDE
    q3 = q_s.astype(_F32).reshape(b2, 1, NSA_W)
    idx8, oc = _s_topk(q3, kcb_s, vcb_s, _cover_matrix(nsel_s, nb16_s, nsel_pad), past, nsel_s)
    n_top = min(SEL_TOPN, nsel_s)
    idx_flat = idx8[:, :KV_GROUPS, :n_top].reshape(-1)
    o_nsa_s = _s_attn(idx_flat, pt_flat, pool_t, q3, gates_s.reshape(b2, 1, LANES), oc,
                      rows_s.reshape(b2, 1, 4 * KV_W), win_s.reshape(b2, 1, 2 * KV_W), cache_win_t,
                      past, nsel_s, past, n_pages).reshape(b2, NSA_W).astype(_MXU)
    nr = b2 * RET_HEADS
    heads = lambda a: a.reshape(nr, HEAD_DIM)
    rq, rk, rv, rg = (heads(ret_s[:, k * RET_W:(k + 1) * RET_W]) for k in range(4))
    gam = jnp.tile(jnp.exp(lg), (b2,))[:, None]
    dd = HEAD_DIM * HEAD_DIM
    lane_d = jnp.arange(dd) // HEAD_DIM
    lane_e = jnp.arange(dd) % HEAD_DIM
    e1 = (jnp.arange(HEAD_DIM)[:, None] == lane_d[None, :]).astype(_MXU)
    e2 = (jnp.arange(HEAD_DIM)[:, None] == lane_e[None, :]).astype(_MXU)
    o_ret_s, st_s = _s_ret(rq, rk, rv, rg, gam, state_ret[l].reshape(nr, dd), e1, e2, min(128, nr))
    y_sample = _post(xs2, o_nsa_s, o_ret_s.reshape(b2, RET_W), wo, ln2r, wg, wu, wd, lnfr, tms).reshape(b2, 1, d)
    kv_sample = jnp.transpose(rows_st.reshape(4, KV_GROUPS, HEAD_DIM, b2), (3, 0, 1, 2))[None, :, None]
    win_sample = jnp.concatenate([cache_win[l][:, 1:], win_s.reshape(b2, 1, 2, KV_GROUPS, HEAD_DIM)], axis=1)[None]
    ret_sample = st_s.reshape(1, b2, RET_HEADS, HEAD_DIM, HEAD_DIM)
    return (y_prompt, y_sample, kv_prompt, kv_sample, win_prompt, win_sample, ret_prompt, ret_sample)
```

```python
import functools

import numpy as np
import jax
import jax.numpy as jnp
from jax import lax
from jax.experimental import pallas as pl
from jax.experimental.pallas import tpu as pltpu

HEAD_DIM = 64
NSA_HEADS = 8
RET_HEADS = 8
KV_GROUPS = 2
HPG = NSA_HEADS // KV_GROUPS
CMP_BLK = 32
CMP_STRIDE = 16
CMP_HID = 4 * HEAD_DIM
SEL_BLK = 64
SEL_TOPN = 16
WINDOW = 512
Q_BLK = 128
RET_CHUNK = 128
PAGE_SIZE = 128
ROPE_THETA = 10000.0
RMS_EPS = 1e-6

LANES = 128
NSA_W = NSA_HEADS * HEAD_DIM
RET_W = RET_HEADS * HEAD_DIM
KV_W = KV_GROUPS * HEAD_DIM
NEG_BIG = -(2.0 ** 100)
VMEM_LIMIT = 56 * 1024 * 1024

_MXU = jnp.bfloat16
_F32 = jnp.float32


def _dot(a, b):
    return jnp.dot(a.astype(_MXU), b.astype(_MXU), preferred_element_type=_F32)


def _dot_nt(a, b):
    return lax.dot_general(a.astype(_MXU), b.astype(_MXU), (((1,), (1,)), ((), ())),
                           preferred_element_type=_F32)


def _dot_tn(a, b):
    return lax.dot_general(a.astype(_MXU), b.astype(_MXU), (((0,), (0,)), ((), ())),
                           preferred_element_type=_F32)


def _split3(x):
    hi = x.astype(_MXU)
    r1 = x - hi.astype(_F32)
    mid = r1.astype(_MXU)
    lo = (r1 - mid.astype(_F32)).astype(_MXU)
    return hi, mid, lo


def _params(sem):
    return pltpu.CompilerParams(dimension_semantics=sem, vmem_limit_bytes=VMEM_LIMIT)


_C_Q, _C_KV, _C_RQ, _C_RK, _C_RV, _C_RG, _C_GT = 0, 512, 1280, 1792, 2304, 2816, 3328
_PROJ_COLS = 3456


def _proj_kernel(x_ref, g_ref, w_ref, cos_ref, sa_ref, sb_ref,
                 q_ref, rows_ref, win_ref, rows_t_ref, win_t_ref, kk_ref, vt_ref, gates_ref, ret_ref):
    x = x_ref[...]
    ms = jnp.mean(x * x, axis=-1, keepdims=True)
    h = (x * lax.rsqrt(ms + RMS_EPS) * g_ref[...]).astype(_MXU)
    cos, sa, sb = cos_ref[...], sa_ref[...], sb_ref[...]

    def seg(c0, n):
        return jnp.dot(h, w_ref[:, c0:c0 + n], preferred_element_type=_F32)

    def rope(p):
        return p * cos + pltpu.roll(p, LANES - 32, 1) * sa + pltpu.roll(p, 32, 1) * sb

    def slab(p, s):
        return p[:, s * LANES:(s + 1) * LANES]

    scale = HEAD_DIM ** -0.5
    p = seg(_C_Q, NSA_W)
    for s in range(4):
        q_ref[:, s * LANES:(s + 1) * LANES] = (rope(slab(p, s)) * scale).astype(q_ref.dtype)
    p = seg(_C_KV, 6 * KV_W)
    kc, vc = rope(slab(p, 0)), slab(p, 1)
    ks, vs = rope(slab(p, 2)), slab(p, 3)
    kw, vw = rope(slab(p, 4)), slab(p, 5)
    rows_ref[:, 0:128] = kc
    rows_ref[:, 128:256] = vc
    rows_ref[:, 256:384] = ks
    rows_ref[:, 384:512] = vs
    win_ref[:, 0:128] = kw
    win_ref[:, 128:256] = vw
    kk_ref[:, 0:128] = ks.astype(kk_ref.dtype)
    kk_ref[:, 128:256] = kw.astype(kk_ref.dtype)
    vs_t, vw_t = vs.T, vw.T
    rows_t_ref[0] = kc.T
    rows_t_ref[1] = vc.T
    rows_t_ref[2] = ks.T
    rows_t_ref[3] = vs_t
    win_t_ref[0] = kw.T
    win_t_ref[1] = vw_t
    vt_ref[0] = vs_t.astype(vt_ref.dtype)
    vt_ref[1] = vw_t.astype(vt_ref.dtype)
    p = seg(_C_RQ, RET_W)
    for s in range(4):
        ret_ref[:, s * LANES:(s + 1) * LANES] = rope(slab(p, s))
    p = seg(_C_RK, RET_W)
    for s in range(4):
        ret_ref[:, RET_W + s * LANES:RET_W + (s + 1) * LANES] = rope(slab(p, s)) * scale
    ret_ref[:, 2 * RET_W:3 * RET_W] = seg(_C_RV, RET_W)
    ret_ref[:, 3 * RET_W:4 * RET_W] = seg(_C_RG, RET_W)
    gates_ref[...] = jax.nn.sigmoid(seg(_C_GT, LANES))


def _proj(x2, ln, w_perm, cos, sa, sb, tm):
    n, d = x2.shape
    tt = cos.shape[0]
    nt = tt // tm
    row = lambda i: (i, 0)
    tab = lambda i: (i % nt, 0)
    const = lambda i: (0, 0)
    out_shape = (
        jax.ShapeDtypeStruct((n, NSA_W), _MXU),
        jax.ShapeDtypeStruct((n, 4 * KV_W), _F32),
        jax.ShapeDtypeStruct((n, 2 * KV_W), _F32),
        jax.ShapeDtypeStruct((n // tt, 4, KV_W, tt), _F32),
        jax.ShapeDtypeStruct((n // tt, 2, KV_W, tt), _F32),
        jax.ShapeDtypeStruct((n, 2 * KV_W), _MXU),
        jax.ShapeDtypeStruct((n // tt, 2, KV_W, tt), _MXU),
        jax.ShapeDtypeStruct((n, LANES), _F32),
        jax.ShapeDtypeStruct((n, 4 * RET_W), _F32),
    )
    return pl.pallas_call(
        _proj_kernel,
        grid=(n // tm,),
        in_specs=[
            pl.BlockSpec((tm, d), row),
            pl.BlockSpec((1, d), const),
            pl.BlockSpec((d, _PROJ_COLS), const),
            pl.BlockSpec((tm, LANES), tab),
            pl.BlockSpec((tm, LANES), tab),
            pl.BlockSpec((tm, LANES), tab),
        ],
        out_specs=(
            pl.BlockSpec((tm, NSA_W), row),
            pl.BlockSpec((tm, 4 * KV_W), row),
            pl.BlockSpec((tm, 2 * KV_W), row),
            pl.BlockSpec((None, 4, KV_W, tm), lambda i: (i // nt, 0, 0, i % nt)),
            pl.BlockSpec((None, 2, KV_W, tm), lambda i: (i // nt, 0, 0, i % nt)),
            pl.BlockSpec((tm, 2 * KV_W), row),
            pl.BlockSpec((None, 2, KV_W, tm), lambda i: (i // nt, 0, 0, i % nt)),
            pl.BlockSpec((tm, LANES), row),
            pl.BlockSpec((tm, 4 * RET_W), row),
        ),
        out_shape=out_shape,
        compiler_params=_params(("parallel",)),
        name="proj",
    )(x2, ln, w_perm, cos, sa, sb)


def _post_kernel(x_ref, on_ref, or_ref, wo_ref, g2_ref, wg_ref, wu_ref, wd_ref, gf_ref, y_ref):
    x = x_ref[...]
    mix = (jnp.dot(on_ref[...], wo_ref[0:NSA_W, :], preferred_element_type=_F32)
           + jnp.dot(or_ref[...], wo_ref[NSA_W:NSA_W + RET_W, :], preferred_element_type=_F32))
    x1 = x + mix
    ms = jnp.mean(x1 * x1, axis=-1, keepdims=True)
    h = (x1 * lax.rsqrt(ms + RMS_EPS) * g2_ref[...]).astype(_MXU)
    a = jax.nn.silu(jnp.dot(h, wg_ref[...], preferred_element_type=_F32))
    a = a * jnp.dot(h, wu_ref[...], preferred_element_type=_F32)
    y = x1 + jnp.dot(a.astype(_MXU), wd_ref[...], preferred_element_type=_F32)
    ms = jnp.mean(y * y, axis=-1, keepdims=True)
    y_ref[...] = y * lax.rsqrt(ms + RMS_EPS) * gf_ref[...]


def _post(x2, o_nsa, o_ret, w_out, ln2, w_gate, w_up, w_down, ln_f, tm):
    n, d = x2.shape
    dff = w_gate.shape[1]
    row = lambda i: (i, 0)
    const = lambda i: (0, 0)
    once = dict(pipeline_mode=pl.Buffered(1))
    return pl.pallas_call(
        _post_kernel,
        grid=(n // tm,),
        in_specs=[
            pl.BlockSpec((tm, d), row),
            pl.BlockSpec((tm, NSA_W), row),
            pl.BlockSpec((tm, RET_W), row),
            pl.BlockSpec((d, d), const, **once),
            pl.BlockSpec((1, d), const),
            pl.BlockSpec((d, dff), const, **once),
            pl.BlockSpec((d, dff), const, **once),
            pl.BlockSpec((dff, d), const, **once),
            pl.BlockSpec((1, d), const),
        ],
        out_specs=pl.BlockSpec((tm, d), row),
        out_shape=jax.ShapeDtypeStruct((n, d), _F32),
        compiler_params=_params(("parallel",)),
        name="post",
    )(x2, o_nsa, o_ret, w_out, ln2, w_gate, w_up, w_down, ln_f)


def _compress_body(src_refs, nb16, pos_ref, w1_ref, w2_ref):
    outs = []
    half = CMP_STRIDE // 2
    for c in range(2):
        a = jnp.zeros((nb16, 2 * CMP_HID), _F32)
        b = jnp.zeros((nb16, 2 * CMP_HID), _F32)
        for tp in range(half):
            t0, t1 = 2 * tp, 2 * tp + 1
            x0 = src_refs[c][pl.ds(t0, nb16, stride=CMP_STRIDE), :]
            x1 = src_refs[c][pl.ds(t1, nb16, stride=CMP_STRIDE), :]
            xa = jnp.concatenate([x0 + pos_ref[c, t0:t0 + 1, :], x1 + pos_ref[c, t1:t1 + 1, :]], axis=1)
            xb = jnp.concatenate([x0 + pos_ref[c, CMP_STRIDE + t0:CMP_STRIDE + t0 + 1, :],
                                  x1 + pos_ref[c, CMP_STRIDE + t1:CMP_STRIDE + t1 + 1, :]], axis=1)
            a = a + _dot(xa, w1_ref[c, tp])
            b = b + _dot(xb, w1_ref[c, half + tp])
        hid = a + pltpu.roll(b, nb16 - 1, 0)
        outs.append(_dot(jax.nn.gelu(hid), w2_ref[c]))
    return outs


def _fill_token_rows(buf_ref, c, r0, plane):
    buf_ref[c, r0:r0 + LANES, :] = plane.T


def _compress_prompt_kernel(rt_ref, pos_ref, w1_ref, w2_ref, kcb_ref, vcbt_ref, buf_ref):
    nb16 = kcb_ref.shape[0]
    for j in range(nb16 * CMP_STRIDE // LANES):
        for c in range(2):
            _fill_token_rows(buf_ref, c, j * LANES, rt_ref[c, :, j * LANES:(j + 1) * LANES])
    kcb, vcb = _compress_body((buf_ref.at[0], buf_ref.at[1]), nb16, pos_ref, w1_ref, w2_ref)
    kcb_ref[...] = kcb
    vcbt_ref[...] = vcb.T


def _compress_prompt(rows_t, pos, w1bd, w2bd):
    b, _, _, t = rows_t.shape
    nb16 = t // CMP_STRIDE
    c3 = lambda i: (0, 0, 0)
    c4 = lambda i: (0, 0, 0, 0)
    out = jax.ShapeDtypeStruct((b, nb16, LANES), _F32)
    return pl.pallas_call(
        _compress_prompt_kernel,
        grid=(b,),
        in_specs=[
            pl.BlockSpec((None, 2, KV_W, t), lambda i: (i, 0, 0, 0)),
            pl.BlockSpec(pos.shape, c3),
            pl.BlockSpec(w1bd.shape, c4),
            pl.BlockSpec(w2bd.shape, c3),
        ],
        out_specs=(pl.BlockSpec((None, nb16, LANES), lambda i: (i, 0, 0)),
                   pl.BlockSpec((None, LANES, nb16), lambda i: (i, 0, 0))),
        out_shape=(out, jax.ShapeDtypeStruct((b, LANES, nb16), _F32)),
        scratch_shapes=[pltpu.VMEM((2, t, LANES), _F32)],
        compiler_params=_params(("parallel",)),
        name="compress_prompt",
    )(rows_t, pos, w1bd, w2bd)


def _compress_sample_kernel(n_pages, pt_ref, *refs):
    page_refs = refs[:n_pages]
    pos_ref, w1_ref, w2_ref, kcb_ref, vcb_ref, buf_ref = refs[n_pages:]
    for j in range(n_pages):
        for c in range(2):
            _fill_token_rows(buf_ref, c, j * PAGE_SIZE, page_refs[j][c])
    nb16 = kcb_ref.shape[0]
    kcb, vcb = _compress_body((buf_ref.at[0], buf_ref.at[1]), nb16, pos_ref, w1_ref, w2_ref)
    kcb_ref[...] = kcb
    vcb_ref[...] = vcb


def _compress_sample(pool_t, page_table_flat, b2, n_pages, pos, w1bd, w2bd):
    past = n_pages * PAGE_SIZE
    nb16 = past // CMP_STRIDE
    c3 = lambda i, pt: (0, 0, 0)
    c4 = lambda i, pt: (0, 0, 0, 0)

    def page_spec(j):
        return pl.BlockSpec((None, 2, KV_W, PAGE_SIZE), lambda i, pt: (pt[i * n_pages + j], 0, 0, 0))

    out = jax.ShapeDtypeStruct((b2, nb16, LANES), _F32)
    grid_spec = pltpu.PrefetchScalarGridSpec(
        num_scalar_prefetch=1,
        grid=(b2,),
        in_specs=[page_spec(j) for j in range(n_pages)] + [
            pl.BlockSpec(pos.shape, c3),
            pl.BlockSpec(w1bd.shape, c4),
            pl.BlockSpec(w2bd.shape, c3),
        ],
        out_specs=(pl.BlockSpec((None, nb16, LANES), lambda i, pt: (i, 0, 0)),
                   pl.BlockSpec((None, nb16, LANES), lambda i, pt: (i, 0, 0))),
        scratch_shapes=[pltpu.VMEM((2, past, LANES), _F32)],
    )
    return pl.pallas_call(
        functools.partial(_compress_sample_kernel, n_pages),
        grid_spec=grid_spec,
        out_shape=(out, out),
        compiler_params=_params(("arbitrary",)),
        name="compress_sample",
    )(page_table_flat, *([pool_t] * n_pages), pos, w1bd, w2bd)


def _group_queries(qf, rows):
    lane = lax.broadcasted_iota(jnp.int32, (rows, LANES), 1)
    out = []
    for h in range(NSA_HEADS):
        g = h // HPG
        s = qf[:, (h // 2) * LANES:(h // 2 + 1) * LANES]
        if (h % 2) != g:
            s = pltpu.roll(s, HEAD_DIM, 1)
        out.append(jnp.where((lane >= g * HEAD_DIM) & (lane < (g + 1) * HEAD_DIM), s, 0.0))
    return out


def _masked_softmax(s, mask):
    s = jnp.where(mask, s, -jnp.inf)
    m = jnp.max(s, axis=-1, keepdims=True)
    m = jnp.where(m > -jnp.inf, m, 0.0)
    e = jnp.where(mask, jnp.exp(s - m), 0.0)
    return e / jnp.maximum(jnp.sum(e, axis=-1, keepdims=True), 1e-30)


def _place_heads(o_heads, rows):
    lane = lax.broadcasted_iota(jnp.int32, (rows, LANES), 1)
    slabs = []
    for k in range(NSA_HEADS // 2):
        pair = []
        for h in (2 * k, 2 * k + 1):
            g = h // HPG
            o = o_heads[h]
            if (h % 2) != g:
                o = pltpu.roll(o, HEAD_DIM, 1)
            pair.append(o)
        slabs.append(jnp.where(lane < HEAD_DIM, pair[0], pair[1]))
    return slabs


def _softmax_keys_parts(s):
    m = jnp.max(s, axis=0, keepdims=True)
    m = jnp.where(m > -jnp.inf, m, 0.0)
    e = jnp.exp(s - m)
    return e, 1.0 / jnp.maximum(jnp.sum(e, axis=0, keepdims=True), 1e-30)


def _softmax_keys(s):
    e, inv = _softmax_keys_parts(s)
    return e * inv


def _nsa_prompt_kernel(nsel, q_ref, gates_ref, kcb_ref, vcbt_ref, kk_ref, vt_ref, eoh_ref, cover_ref,
                       o_ref, acc_ref, m_ref, l_ref, s_ref):
    i = pl.program_id(1)
    tq = Q_BLK
    ncol = NSA_HEADS * tq
    qf = q_ref[...].astype(_F32)
    q2 = jnp.concatenate(_group_queries(qf, tq), axis=0).astype(_MXU)
    qpos_1 = i * tq + lax.broadcasted_iota(jnp.int32, (1, tq), 1)

    def per_head(x):
        return jnp.concatenate([x] * NSA_HEADS, axis=1)

    nb16 = kcb_ref.shape[0]
    blk_end = lax.broadcasted_iota(jnp.int32, (nb16, 1), 0) * CMP_STRIDE + (CMP_BLK - 1)
    bias_c = jnp.where(blk_end <= qpos_1, 0.0, -jnp.inf)
    p_c = _softmax_keys(_dot_nt(kcb_ref[...], q2) + per_head(bias_c))
    o_c = _dot(vcbt_ref[...], p_c)

    jidx = lax.broadcasted_iota(jnp.int32, (nsel, tq), 0)
    qpos_l = i * tq + lax.broadcasted_iota(jnp.int32, (nsel, tq), 1)
    cur = qpos_l // SEL_BLK
    forced = (jidx == 0) | (jidx == cur) | (jidx == cur - 1)
    valid = jidx * SEL_BLK <= qpos_l
    n_top = min(SEL_TOPN, nsel)
    sub = lax.broadcasted_iota(jnp.int32, (8, tq), 0)
    q_sel = []
    for g in range(KV_GROUPS):
        psum = p_c[:, (g * HPG) * tq:(g * HPG + 1) * tq]
        for hh in range(1, HPG):
            psum = psum + p_c[:, (g * HPG + hh) * tq:(g * HPG + hh + 1) * tq]
        imp_t = jnp.zeros((nsel, tq), _F32)
        for part in _split3(psum):
            imp_t = imp_t + jnp.dot(cover_ref[...], part, preferred_element_type=_F32)
        score = jnp.where(forced, jnp.inf, jnp.where(valid, imp_t, -jnp.inf))
        nv = nsel // 8
        sc_v = [score[8 * v:8 * v + 8] for v in range(nv)]
        rank_v = [jnp.zeros((8, tq), jnp.int32) for _ in range(nv)]
        for jp in range(nsel):
            rowb = jnp.broadcast_to(score[jp:jp + 1, :], (8, tq))
            for v in range(nv):
                if v > jp // 8:
                    beats = jnp.where(rowb >= sc_v[v], 1, 0)
                elif v < jp // 8:
                    beats = jnp.where(rowb > sc_v[v], 1, 0)
                else:
                    beats = jnp.where(sub > (jp % 8), jnp.where(rowb >= sc_v[v], 1, 0),
                                      jnp.where(rowb > sc_v[v], 1, 0))
                rank_v[v] = rank_v[v] + beats
        sel_t = jnp.concatenate([jnp.where(r < n_top, 1.0, 0.0) for r in rank_v], axis=0)
        if nsel < LANES:
            sel_t = jnp.concatenate([sel_t, jnp.ones((LANES - nsel, tq), _F32)], axis=0)
        sel = sel_t.T
        bias = ((sel - 1.0) * (-NEG_BIG)).astype(_MXU)
        q_sel.extend([bias] * HPG)
    q_sel = jnp.concatenate(q_sel, axis=0)
    q_aug = jnp.concatenate([q2, q_sel], axis=1)

    tk = 2 * tq
    acc_ref[...] = jnp.zeros(acc_ref.shape, _F32)
    m_ref[...] = jnp.full(m_ref.shape, 2.0 * NEG_BIG, _F32)
    l_ref[...] = jnp.zeros(l_ref.shape, _F32)

    def scores(kt):
        k0 = pl.multiple_of(kt * tk, tk)
        k_aug = jnp.concatenate([kk_ref[pl.ds(k0, tk), 0:LANES], eoh_ref[pl.ds(k0, tk), :]], axis=1)
        return _dot_nt(k_aug, q_aug)

    def consume(slot, kt, causal):
        k0 = pl.multiple_of(kt * tk, tk)
        s = s_ref[slot]
        if causal:
            kpos = k0 + lax.broadcasted_iota(jnp.int32, (tk, 1), 0)
            s = s + per_head(jnp.where(kpos <= qpos_1, 0.0, NEG_BIG))
        m_old = m_ref[...]
        m_new = jnp.maximum(m_old, jnp.max(s, axis=0, keepdims=True))
        alpha = jnp.exp(m_old - m_new)
        p = jnp.exp(s - m_new)
        m_ref[...] = m_new
        l_ref[...] = alpha * l_ref[...] + jnp.sum(p, axis=0, keepdims=True)
        acc_ref[...] = alpha * acc_ref[...] + _dot(vt_ref[0, :, pl.ds(k0, tk)], p)

    last = (i + 2) // 2 - 1
    s_ref[0] = scores(0)

    def tile_pair(j, carry):
        s_ref[1] = scores(2 * j + 1)
        consume(0, 2 * j, False)
        s_ref[0] = scores(2 * j + 2)
        consume(1, 2 * j + 1, False)
        return carry

    lax.fori_loop(0, last // 2, tile_pair, 0)

    @pl.when(last % 2 == 1)
    def _():
        s_ref[1] = scores(last)
        consume(0, last - 1, False)
        consume(1, last, True)

    @pl.when(last % 2 == 0)
    def _():
        consume(0, last, True)

    o_s = acc_ref[...] * (1.0 / l_ref[...])

    wk = WINDOW + tq
    start = pl.multiple_of(jnp.maximum(i * tq - WINDOW, 0), tq)
    wpos = start + lax.broadcasted_iota(jnp.int32, (wk, 1), 0)
    bias_w = jnp.where(wpos <= qpos_1, jnp.where(wpos > qpos_1 - WINDOW, 0.0, -jnp.inf), -jnp.inf)
    e_w, inv_w = _softmax_keys_parts(_dot_nt(kk_ref[pl.ds(start, wk), LANES:2 * LANES], q2) + per_head(bias_w))
    o_w = _dot(vt_ref[1, :, pl.ds(start, wk)], e_w) * inv_w

    gt_t = gates_ref[...].T
    o_rows = []
    for h in range(NSA_HEADS):
        g = h // HPG
        rs = slice(g * HEAD_DIM, (g + 1) * HEAD_DIM)
        cs = slice(h * tq, (h + 1) * tq)
        o_rows.append(gt_t[3 * h:3 * h + 1] * o_c[rs, cs] + gt_t[3 * h + 1:3 * h + 2] * o_s[rs, cs]
                      + gt_t[3 * h + 2:3 * h + 3] * o_w[rs, cs])
    o_ref[...] = jnp.concatenate(o_rows, axis=0).T.astype(o_ref.dtype)


def _nsa_prompt(q, gates, kcb, vcbt, kk3, vt4, eoh, cover, b, t):
    nq = t // Q_BLK
    nb16 = t // CMP_STRIDE
    nsel = t // SEL_BLK
    rowblk = lambda bi, i: (bi * nq + i, 0)
    per_b = lambda bi, i: (bi, 0, 0)
    const = lambda bi, i: (0, 0)
    return pl.pallas_call(
        functools.partial(_nsa_prompt_kernel, nsel),
        grid=(b, nq),
        in_specs=[
            pl.BlockSpec((Q_BLK, NSA_W), rowblk),
            pl.BlockSpec((Q_BLK, LANES), rowblk),
            pl.BlockSpec((None, nb16, LANES), per_b),
            pl.BlockSpec((None, LANES, nb16), per_b),
            pl.BlockSpec((None, t, 2 * KV_W), per_b),
            pl.BlockSpec((None, 2, KV_W, t), lambda bi, i: (bi, 0, 0, 0)),
            pl.BlockSpec((t, LANES), const),
            pl.BlockSpec((nsel, nb16), const),
        ],
        out_specs=pl.BlockSpec((Q_BLK, NSA_W), rowblk),
        out_shape=jax.ShapeDtypeStruct((b * t, NSA_W), _MXU),
        scratch_shapes=[pltpu.VMEM((KV_W, NSA_HEADS * Q_BLK), _F32),
                        pltpu.VMEM((1, NSA_HEADS * Q_BLK), _F32),
                        pltpu.VMEM((1, NSA_HEADS * Q_BLK), _F32),
                        pltpu.VMEM((2, 2 * Q_BLK, NSA_HEADS * Q_BLK), _F32)],
        compiler_params=_params(("parallel", "arbitrary")),
        name="nsa_prompt",
    )(q, gates, kcb, vcbt, kk3, vt4, eoh, cover)


def _ret_prompt_kernel(ret_ref, dmat_ref, rowdec_ref, kdec_ref, sdec_ref, bmask_ref,
                       o_ref, st_ref, s_ref):
    c = pl.program_id(1)
    nc = pl.num_programs(1)
    tq = RET_CHUNK

    @pl.when(c == 0)
    def _():
        s_ref[...] = jnp.zeros(s_ref.shape, _F32)

    lane = lax.broadcasted_iota(jnp.int32, (tq, LANES), 1)
    lo = lane < HEAD_DIM
    bmask = bmask_ref[...]
    for pr in range(RET_HEADS // 2):
        c0 = pr * LANES
        q = ret_ref[:, c0:c0 + LANES]
        k = ret_ref[:, RET_W + c0:RET_W + c0 + LANES]
        v = ret_ref[:, 2 * RET_W + c0:2 * RET_W + c0 + LANES]
        g = ret_ref[:, 3 * RET_W + c0:3 * RET_W + c0 + LANES]
        q2 = jnp.concatenate([jnp.where(lo, q, 0.0), jnp.where(lo, 0.0, q)], axis=0)
        att = _dot_nt(q2, k) * jnp.concatenate([dmat_ref[2 * pr], dmat_ref[2 * pr + 1]], axis=0)
        o2 = _dot(att, v)
        o = jnp.where(lo, o2[0:tq], o2[tq:2 * tq])
        s_old = s_ref[pr]
        o = o + _dot(q, s_old) * rowdec_ref[pr]
        s_new = s_old * sdec_ref[pr] + _dot_tn(k * kdec_ref[pr], v) * bmask
        s_ref[pr] = s_new
        o_sq = o * o
        s0 = jnp.sum(jnp.where(lo, o_sq, 0.0), axis=-1, keepdims=True)
        s1 = jnp.sum(jnp.where(lo, 0.0, o_sq), axis=-1, keepdims=True)
        ms = jnp.where(lo, s0, s1) * (1.0 / HEAD_DIM)
        o = o * lax.rsqrt(ms + RMS_EPS) * jax.nn.silu(g)
        o_ref[:, c0:c0 + LANES] = o.astype(o_ref.dtype)

    @pl.when(c == nc - 1)
    def _():
        for pr in range(RET_HEADS // 2):
            s_fin = s_ref[pr]
            st_ref[2 * pr] = s_fin[0:HEAD_DIM, 0:HEAD_DIM]
            st_ref[2 * pr + 1] = pltpu.roll(s_fin, HEAD_DIM, 1)[HEAD_DIM:2 * HEAD_DIM, 0:HEAD_DIM]


def _ret_prompt(ret, tabs, b, t):
    nc = t // RET_CHUNK
    rowblk = lambda bi, i: (bi * nc + i, 0)
    c3 = lambda bi, i: (0, 0, 0)
    dmat, rowdec, kdec, sdec, bmask = tabs
    return pl.pallas_call(
        _ret_prompt_kernel,
        grid=(b, nc),
        in_specs=[
            pl.BlockSpec((RET_CHUNK, 4 * RET_W), rowblk),
            pl.BlockSpec(dmat.shape, c3),
            pl.BlockSpec(rowdec.shape, c3),
            pl.BlockSpec(kdec.shape, c3),
            pl.BlockSpec(sdec.shape, c3),
            pl.BlockSpec(bmask.shape, lambda bi, i: (0, 0)),
        ],
        out_specs=(pl.BlockSpec((RET_CHUNK, RET_W), rowblk),
                   pl.BlockSpec((None, RET_HEADS, HEAD_DIM, HEAD_DIM), lambda bi, i: (bi, 0, 0, 0))),
        out_shape=(jax.ShapeDtypeStruct((b * t, RET_W), _MXU),
                   jax.ShapeDtypeStruct((b, RET_HEADS, HEAD_DIM, HEAD_DIM), _F32)),
        scratch_shapes=[pltpu.VMEM((RET_HEADS // 2, LANES, LANES), _F32)],
        compiler_params=_params(("parallel", "arbitrary")),
        name="ret_prompt",
    )(ret, dmat, rowdec, kdec, sdec, bmask)


def _sample_queries(q_row):
    heads = _group_queries(q_row.astype(_F32), 1)
    row = lax.broadcasted_iota(jnp.int32, (NSA_HEADS, LANES), 0)
    q8 = jnp.zeros((NSA_HEADS, LANES), _F32)
    for h in range(NSA_HEADS):
        q8 = jnp.where(row == h, jnp.broadcast_to(heads[h], (NSA_HEADS, LANES)), q8)
    return q8


def _s_topk_kernel(q_pos, nsel, q_ref, kcb_ref, vcb_ref, cover_ref, idx_ref, oc_ref):
    nseq, nb16, _ = kcb_ref.shape
    nrow = nseq * KV_GROUPS
    blk_end = lax.broadcasted_iota(jnp.int32, (NSA_HEADS, nb16), 1) * CMP_STRIDE + (CMP_BLK - 1)
    row = lax.broadcasted_iota(jnp.int32, (nrow, nb16), 0)
    psum = jnp.zeros((nrow, nb16), _F32)
    for s in range(nseq):
        q8 = _sample_queries(q_ref[s])
        p_c = _masked_softmax(_dot_nt(q8, kcb_ref[s]), blk_end <= q_pos)
        oc_ref[s] = _dot(p_c, vcb_ref[s])
        for g in range(KV_GROUPS):
            acc = p_c[g * HPG:g * HPG + 1]
            for hh in range(1, HPG):
                acc = acc + p_c[g * HPG + hh:g * HPG + hh + 1]
            psum = jnp.where(row == KV_GROUPS * s + g, jnp.broadcast_to(acc, (nrow, nb16)), psum)

    npad = cover_ref.shape[0]
    imp = jnp.zeros((nrow, npad), _F32)
    for part in _split3(psum):
        imp = imp + lax.dot_general(part, cover_ref[...], (((1,), (1,)), ((), ())),
                                    preferred_element_type=_F32)
    j = lax.broadcasted_iota(jnp.int32, (nrow, npad), 1)
    cur = q_pos // SEL_BLK
    forced = (j == 0) | (j == cur) | (j == cur - 1)
    valid = j * SEL_BLK <= q_pos
    score = jnp.where(forced, jnp.inf, jnp.where(valid, imp, -jnp.inf))
    jf = j.astype(_F32)
    alive = jnp.where(j < nsel, 1.0, 0.0)
    lane = lax.broadcasted_iota(jnp.int32, (nrow, LANES), 1)
    idx = jnp.zeros((nrow, LANES), _F32)
    for r in range(min(SEL_TOPN, nsel)):
        live = alive > 0.0
        m = jnp.max(jnp.where(live, score, -jnp.inf), axis=-1, keepdims=True)
        cand = jnp.where(live, jnp.where(score == m, jf, float(npad)), float(npad))
        jmin = jnp.min(cand, axis=-1, keepdims=True)
        idx = jnp.where(lane == r, jmin, idx)
        alive = jnp.where(jf == jmin, 0.0, alive)
    idx_ref[...] = idx.astype(jnp.int32)


def _s_topk(q3, kcb, vcb, cover_s, q_pos, nsel):
    b2 = q3.shape[0]
    nb16 = kcb.shape[1]
    nseq = 8 if b2 % 8 == 0 else b2
    per_b = lambda i: (i, 0, 0)
    return pl.pallas_call(
        functools.partial(_s_topk_kernel, q_pos, nsel),
        grid=(b2 // nseq,),
        in_specs=[
            pl.BlockSpec((nseq, 1, NSA_W), per_b),
            pl.BlockSpec((nseq, nb16, LANES), per_b),
            pl.BlockSpec((nseq, nb16, LANES), per_b),
            pl.BlockSpec(cover_s.shape, lambda i: (0, 0)),
        ],
        out_specs=(pl.BlockSpec((nseq * KV_GROUPS, LANES), lambda i: (i, 0)),
                   pl.BlockSpec((nseq, NSA_HEADS, LANES), per_b)),
        out_shape=(jax.ShapeDtypeStruct((b2 * KV_GROUPS, LANES), jnp.int32),
                   jax.ShapeDtypeStruct((b2, NSA_HEADS, LANES), _F32)),
        compiler_params=_params(("parallel",)),
        name="sample_topk",
    )(q3, kcb, vcb, cover_s)


def _s_attn_kernel(q_pos, nsel, past, wbuf, idx_ref, pt_ref, *refs):
    del pt_ref
    n_top = min(SEL_TOPN, nsel)
    nblk = KV_GROUPS * n_top
    blk_refs = refs[:nblk]
    q_ref, gates_ref, oc_ref, rows_ref, win_ref, cw_ref, o_ref = refs[nblk:]
    b = pl.program_id(0)
    q8 = _sample_queries(q_ref[...])
    row1 = lax.broadcasted_iota(jnp.int32, (NSA_HEADS, 1), 0)
    is_g0 = row1 < HPG

    nk = n_top * PAGE_SIZE
    lane_k = lax.broadcasted_iota(jnp.int32, (1, nk), 1)
    kslot = lane_k // PAGE_SIZE
    khalf = (lane_k % PAGE_SIZE) // SEL_BLK
    o_s = None
    for g in range(KV_GROUPS):
        kcat = jnp.concatenate([blk_refs[g * n_top + k][0] for k in range(n_top)], axis=1)
        vcat = jnp.concatenate([blk_refs[g * n_top + k][1] for k in range(n_top)], axis=1)
        s = _dot(q8, kcat)
        bias = jnp.full((1, nk), NEG_BIG, _F32)
        has_new = jnp.zeros((1, 1), _F32)
        for k in range(n_top):
            jk = idx_ref[(b * KV_GROUPS + g) * n_top + k]
            is_new = jk == nsel - 1
            half = jnp.where(is_new, -1, jk % 2)
            bias = jnp.where((kslot == k) & (khalf == half), 0.0, bias)
            has_new = jnp.where(is_new, 1.0, has_new)
        s = s + bias
        k_new = rows_ref[:, 2 * LANES:3 * LANES]
        v_new = rows_ref[:, 3 * LANES:4 * LANES]
        s_new = jnp.sum(q8 * k_new, axis=-1, keepdims=True) + jnp.where(has_new > 0.0, 0.0, NEG_BIG)
        m = jnp.maximum(jnp.max(s, axis=-1, keepdims=True), s_new)
        e = jnp.exp(s - m)
        e_new = jnp.exp(s_new - m)
        den = jnp.sum(e, axis=-1, keepdims=True) + e_new
        o_g = (_dot_nt(e, vcat) + e_new * v_new) / den
        o_s = o_g if o_s is None else jnp.where(is_g0, o_s, o_g)

    s_w = _dot(q8, cw_ref[0])
    wpos = (past - wbuf) + lax.broadcasted_iota(jnp.int32, (1, wbuf), 1)
    wmask = (wpos <= q_pos) & (wpos > q_pos - WINDOW) & (wpos >= 0)
    s_w = jnp.where(wmask, s_w, NEG_BIG)
    s_wn = jnp.sum(q8 * win_ref[:, 0:LANES], axis=-1, keepdims=True)
    m = jnp.maximum(jnp.max(s_w, axis=-1, keepdims=True), s_wn)
    e = jnp.exp(s_w - m)
    e_new = jnp.exp(s_wn - m)
    den = jnp.sum(e, axis=-1, keepdims=True) + e_new
    o_w = (_dot_nt(e, cw_ref[1]) + e_new * win_ref[:, LANES:2 * LANES]) / den

    gt = jnp.broadcast_to(gates_ref[...], (NSA_HEADS, LANES))
    lane = lax.broadcasted_iota(jnp.int32, (NSA_HEADS, LANES), 1)
    row = lax.broadcasted_iota(jnp.int32, (NSA_HEADS, LANES), 0)
    o = jnp.zeros((NSA_HEADS, LANES), _F32)
    for jb, ob in enumerate((oc_ref[...], o_s, o_w)):
        gcol = jnp.sum(jnp.where(lane == 3 * row + jb, gt, 0.0), axis=-1, keepdims=True)
        o = o + gcol * ob
    heads = [o[h:h + 1] for h in range(NSA_HEADS)]
    for k, s in enumerate(_place_heads(heads, 1)):
        o_ref[:, k * LANES:(k + 1) * LANES] = s.astype(o_ref.dtype)


def _s_attn(idx_flat, pt_flat, pool_t, q3, gates3, oc, rows3, win3, cache_win_t, q_pos, nsel, past, n_pages):
    b2 = q3.shape[0]
    wbuf = cache_win_t.shape[3]
    n_top = min(SEL_TOPN, nsel)
    per_b = lambda i, idx, pt: (i, 0, 0)

    def blk_spec(g, k):
        def imap(i, idx, pt):
            j = jnp.minimum(idx[(i * KV_GROUPS + g) * n_top + k], nsel - 2)
            return (pt[i * n_pages + j // 2], 1, 0, 0)
        return pl.BlockSpec((None, 2, KV_W, PAGE_SIZE), imap)

    grid_spec = pltpu.PrefetchScalarGridSpec(
        num_scalar_prefetch=2,
        grid=(b2,),
        in_specs=[blk_spec(g, k) for g in range(KV_GROUPS) for k in range(n_top)] + [
            pl.BlockSpec((None, 1, NSA_W), per_b),
            pl.BlockSpec((None, 1, LANES), per_b),
            pl.BlockSpec((None, NSA_HEADS, LANES), per_b),
            pl.BlockSpec((None, 1, 4 * KV_W), per_b),
            pl.BlockSpec((None, 1, 2 * KV_W), per_b),
            pl.BlockSpec((None, 2, KV_W, wbuf), lambda i, idx, pt: (i, 0, 0, 0)),
        ],
        out_specs=pl.BlockSpec((None, 1, NSA_W), per_b),
    )
    return pl.pallas_call(
        functools.partial(_s_attn_kernel, q_pos, nsel, past, wbuf),
        grid_spec=grid_spec,
        out_shape=jax.ShapeDtypeStruct((b2, 1, NSA_W), _F32),
        compiler_params=_params(("arbitrary",)),
        name="sample_attn",
    )(idx_flat, pt_flat, *([pool_t] * (KV_GROUPS * n_top)), q3, gates3, oc, rows3, win3, cache_win_t)


def _s_ret_kernel(q_ref, k_ref, v_ref, g_ref, gam_ref, st_ref, e1_ref, e2_ref, o_ref, sn_ref):
    q, k, v = q_ref[...], k_ref[...], v_ref[...]
    gam = gam_ref[...]
    st = st_ref[...]
    q_exp = _dot(q, e1_ref[...])
    k_exp = _dot(k, e1_ref[...])
    v_exp = _dot(v, e2_ref[...])
    prod = q_exp * st.astype(_MXU).astype(_F32)
    qs = jnp.zeros(q.shape, _F32)
    for part in _split3(prod):
        qs = qs + lax.dot_general(part, e2_ref[...], (((1,), (1,)), ((), ())), preferred_element_type=_F32)
    qr = q.astype(_MXU).astype(_F32)
    kr = k.astype(_MXU).astype(_F32)
    att = jnp.sum(qr * kr, axis=-1, keepdims=True)
    o = att.astype(_MXU).astype(_F32) * v.astype(_MXU).astype(_F32) + qs * gam
    sn_ref[...] = st * gam + k_exp * v_exp
    ms = jnp.mean(o * o, axis=-1, keepdims=True)
    o_ref[...] = (o * lax.rsqrt(ms + RMS_EPS) * jax.nn.silu(g_ref[...])).astype(o_ref.dtype)


def _s_ret(q, k, v, g, gam, st2, e1, e2, tr):
    n = q.shape[0]
    dd = HEAD_DIM * HEAD_DIM
    row = lambda i: (i, 0)
    const = lambda i: (0, 0)
    return pl.pallas_call(
        _s_ret_kernel,
        grid=(n // tr,),
        in_specs=[
            pl.BlockSpec((tr, HEAD_DIM), row),
            pl.BlockSpec((tr, HEAD_DIM), row),
            pl.BlockSpec((tr, HEAD_DIM), row),
            pl.BlockSpec((tr, HEAD_DIM), row),
            pl.BlockSpec((tr, 1), row),
            pl.BlockSpec((tr, dd), row),
            pl.BlockSpec((HEAD_DIM, dd), const),
            pl.BlockSpec((HEAD_DIM, dd), const),
        ],
        out_specs=(pl.BlockSpec((tr, HEAD_DIM), row), pl.BlockSpec((tr, dd), row)),
        out_shape=(jax.ShapeDtypeStruct((n, HEAD_DIM), _MXU), jax.ShapeDtypeStruct((n, dd), _F32)),
        compiler_params=_params(("parallel",)),
        name="sample_ret",
    )(q, k, v, g, gam, st2, e1, e2)


def _rope_tables(pos):
    half = HEAD_DIM // 2
    inv = 1.0 / (ROPE_THETA ** (jnp.arange(half, dtype=_F32) / half))
    ang = pos.astype(_F32)[:, None] * inv[None, :]
    cos, sin = jnp.cos(ang), jnp.sin(ang)
    zero = jnp.zeros_like(sin)
    reps = LANES // HEAD_DIM
    cos_t = jnp.tile(cos, (1, 2 * reps))
    sa = jnp.tile(jnp.concatenate([-sin, zero], axis=1), (1, reps))
    sb = jnp.tile(jnp.concatenate([zero, sin], axis=1), (1, reps))
    return cos_t, sa, sb


def _cover_matrix(nsel, nb16, nsel_pad):
    c0 = np.arange(nb16) * CMP_STRIDE
    s0 = np.arange(nsel_pad) * SEL_BLK
    m = (c0[None, :] < s0[:, None] + SEL_BLK) & (c0[None, :] + CMP_BLK > s0[:, None])
    m = m & (np.arange(nsel_pad)[:, None] < nsel)
    return jnp.asarray(m.astype(np.float32), dtype=_MXU)


def _retention_tables():
    lg = jnp.log(1.0 - 2.0 ** (-5.0 - jnp.arange(RET_HEADS, dtype=_F32)))
    c = RET_CHUNK
    i = jnp.arange(c, dtype=_F32)
    diff = i[:, None] - i[None, :]
    causal = diff >= 0
    dmat = jnp.where(causal[None], jnp.exp(jnp.where(causal, diff, 0.0)[None] * lg[:, None, None]), 0.0)
    lane_head = jnp.arange(LANES) // HEAD_DIM
    pair_lg = lg.reshape(RET_HEADS // 2, 2)[:, lane_head]
    rowdec = jnp.exp((i + 1.0)[None, :, None] * pair_lg[:, None, :])
    kdec = jnp.exp((c - 1.0 - i)[None, :, None] * pair_lg[:, None, :])
    sdec = jnp.broadcast_to(jnp.exp(c * pair_lg)[:, :, None], (RET_HEADS // 2, LANES, LANES))
    bmask = (lane_head[:, None] == lane_head[None, :]).astype(_F32)
    return lg, (dmat, rowdec, kdec, sdec, bmask)


def _compress_weights(pos, w1, w2):
    w1t = w1.reshape(CMP_BLK, HEAD_DIM, CMP_HID)
    z1 = jnp.zeros_like(w1t)
    w1bd = jnp.concatenate([jnp.concatenate([w1t, z1], axis=2), jnp.concatenate([z1, w1t], axis=2)], axis=1)
    z2 = jnp.zeros_like(w2)
    w2bd = jnp.concatenate([jnp.concatenate([w2, z2], axis=1), jnp.concatenate([z2, w2], axis=1)], axis=0)
    w1pair = w1bd.reshape(CMP_BLK // 2, 2 * LANES, 2 * CMP_HID)
    return jnp.tile(pos, (1, 2)), w1pair.astype(_MXU), w2bd.astype(_MXU)


def kernel(x_prompt, x_sample, cache_kv, cache_win, state_ret, page_table, ln1, w_in, cmp_pos_k, cmp_w1_k,
           cmp_w2_k, cmp_pos_v, cmp_w1_v, cmp_w2_v, w_out, ln2, w_gate, w_up, w_down, ln_f):
    depth = ln1.shape[0]
    assert depth == 1, "single-layer step"
    b, t, d = x_prompt.shape
    b2, s_s, _ = x_sample.shape
    assert s_s == 1
    n_pages = page_table.shape[1]
    past = n_pages * PAGE_SIZE
    wbuf = cache_win.shape[2]
    assert t % (2 * Q_BLK) == 0 and t >= WINDOW + Q_BLK and (t // SEL_BLK) % 8 == 0
    l = 0

    w = w_in[l]
    o_kv = NSA_W
    o_gt = o_kv + 6 * KV_W
    o_r = o_gt + NSA_HEADS * 3
    w_perm = jnp.concatenate([
        w[:, :o_gt], w[:, o_r:], w[:, o_gt:o_r],
        jnp.zeros((d, LANES - NSA_HEADS * 3), w.dtype)], axis=1).astype(_MXU)
    pos_k, w1k, w2k = _compress_weights(cmp_pos_k[l], cmp_w1_k[l], cmp_w2_k[l])
    pos_v, w1v, w2v = _compress_weights(cmp_pos_v[l], cmp_w1_v[l], cmp_w2_v[l])
    cpos = jnp.stack([pos_k, pos_v])
    cw1 = jnp.stack([w1k, w1v])
    cw2 = jnp.stack([w2k, w2v])
    wo, wg, wu, wd = (a[l].astype(_MXU) for a in (w_out, w_gate, w_up, w_down))
    ln1r, ln2r, lnfr = ln1[l][None, :], ln2[l][None, :], ln_f[None, :]
    lg, ret_tabs = _retention_tables()

    tm = 512 if t % 512 == 0 else 256
    xp2 = x_prompt.reshape(b * t, d)
    q, _, _, rows_t, win_t, kk, vt, gates, ret = _proj(xp2, ln1r, w_perm, *_rope_tables(jnp.arange(t)), tm)
    kcb, vcbt = _compress_prompt(rows_t, cpos, cw1, cw2)
    nsel_p = t // SEL_BLK
    nb16_p = t // CMP_STRIDE
    eoh = (jnp.arange(t)[:, None] // SEL_BLK == jnp.arange(LANES)[None, :]).astype(_MXU)
    o_nsa = _nsa_prompt(q, gates, kcb, vcbt, kk.reshape(b, t, 2 * KV_W), vt, eoh,
                        _cover_matrix(nsel_p, nb16_p, nsel_p), b, t)
    o_ret, st_p = _ret_prompt(ret, ret_tabs, b, t)
    y_prompt = _post(xp2, o_nsa, o_ret, wo, ln2r, wg, wu, wd, lnfr, tm).reshape(b, t, d)
    to_cache = lambda a: jnp.transpose(a.reshape(a.shape[0], a.shape[1], KV_GROUPS, HEAD_DIM, a.shape[3]),
                                       (0, 4, 1, 2, 3))[None]
    kv_prompt = to_cache(rows_t)
    win_keep = min(WINDOW, t)
    win_prompt = to_cache(win_t[:, :, :, t - win_keep:])
    ret_prompt = st_p[None]

    xs2 = x_sample.reshape(b2, d)
    tms = min(tm, b2)
    pos_s = jnp.full((b2,), past, jnp.int32)
    q_s, rows_s, win_s, rows_st, _, _, _, gates_s, ret_s = _proj(xs2, ln1r, w_perm, *_rope_tables(pos_s), tms)
    n_pool = cache_kv.shape[1]
    pool_t = jnp.transpose(cache_kv[l], (0, 2, 3, 4, 1)).reshape(n_pool, 4, KV_W, PAGE_SIZE)
    cache_win_t = jnp.transpose(cache_win[l], (0, 2, 3, 4, 1)).reshape(b2, 2, KV_W, wbuf)
    pt_flat = page_table.reshape(-1).astype(jnp.int32)
    kcb_s, vcb_s = _compress_sample(pool_t, pt_flat, b2, n_pages, cpos, cw1, cw2)
    seq_len = past + 1
    nsel_s = -(-seq_len // SEL_BLK)
    nsel_pad = -(-nsel_s // LANES) * LANES
    nb16_s = past // CMP_STRIDE
    q3 = q_s.astype(_F32).reshape(b2, 1, NSA_W)
    idx8, oc = _s_topk(q3, kcb_s, vcb_s, _cover_matrix(nsel_s, nb16_s, nsel_pad), past, nsel_s)
    n_top = min(SEL_TOPN, nsel_s)
    idx_flat = idx8[:, :n_top].reshape(-1)
    o_nsa_s = _s_attn(idx_flat, pt_flat, pool_t, q3, gates_s.reshape(b2, 1, LANES), oc,
                      rows_s.reshape(b2, 1, 4 * KV_W), win_s.reshape(b2, 1, 2 * KV_W), cache_win_t,
                      past, nsel_s, past, n_pages).reshape(b2, NSA_W).astype(_MXU)
    nr = b2 * RET_HEADS
    heads = lambda a: a.reshape(nr, HEAD_DIM)
    rq, rk, rv, rg = (heads(ret_s[:, k * RET_W:(k + 1) * RET_W]) for k in range(4))
    gam = jnp.tile(jnp.exp(lg), (b2,))[:, None]
    dd = HEAD_DIM * HEAD_DIM
    lane_d = jnp.arange(dd) // HEAD_DIM
    lane_e = jnp.arange(dd) % HEAD_DIM
    e1 = (jnp.arange(HEAD_DIM)[:, None] == lane_d[None, :]).astype(_MXU)
    e2 = (jnp.arange(HEAD_DIM)[:, None] == lane_e[None, :]).astype(_MXU)
    o_ret_s, st_s = _s_ret(rq, rk, rv, rg, gam, state_ret[l].reshape(nr, dd), e1, e2, min(128, nr))
    y_sample = _post(xs2, o_nsa_s, o_ret_s.reshape(b2, RET_W), wo, ln2r, wg, wu, wd, lnfr, tms).reshape(b2, 1, d)
    kv_sample = jnp.transpose(rows_st.reshape(4, KV_GROUPS, HEAD_DIM, b2), (3, 0, 1, 2))[None, :, None]
    win_sample = jnp.concatenate([cache_win[l][:, 1:], win_s.reshape(b2, 1, 2, KV_GROUPS, HEAD_DIM)], axis=1)[None]
    ret_sample = st_s.reshape(1, b2, RET_HEADS, HEAD_DIM, HEAD_DIM)
    return (y_prompt, y_sample, kv_prompt, kv_sample, win_prompt, win_sample, ret_prompt, ret_sample)
```

```python
import functools

import numpy as np
import jax
import jax.numpy as jnp
from jax import lax
from jax.experimental import pallas as pl
from jax.experimental.pallas import tpu as pltpu

HEAD_DIM = 64
NSA_HEADS = 8
RET_HEADS = 8
KV_GROUPS = 2
HPG = NSA_HEADS // KV_GROUPS
CMP_BLK = 32
CMP_STRIDE = 16
CMP_HID = 4 * HEAD_DIM
SEL_BLK = 64
SEL_TOPN = 16
WINDOW = 512
Q_BLK = 128
RET_CHUNK = 128
PAGE_SIZE = 128
ROPE_THETA = 10000.0
RMS_EPS = 1e-6

LANES = 128
NSA_W = NSA_HEADS * HEAD_DIM
RET_W = RET_HEADS * HEAD_DIM
KV_W = KV_GROUPS * HEAD_DIM
NEG_BIG = -(2.0 ** 100)
LOG2E = 1.4426950408889634
VT_ROWS = KV_W + 16
VMEM_LIMIT = 56 * 1024 * 1024

_MXU = jnp.bfloat16
_F32 = jnp.float32


def _dot(a, b):
    return jnp.dot(a.astype(_MXU), b.astype(_MXU), preferred_element_type=_F32)


def _dot_nt(a, b):
    return lax.dot_general(a.astype(_MXU), b.astype(_MXU), (((1,), (1,)), ((), ())),
                           preferred_element_type=_F32)


def _dot_tn(a, b):
    return lax.dot_general(a.astype(_MXU), b.astype(_MXU), (((0,), (0,)), ((), ())),
                           preferred_element_type=_F32)


def _split3(x):
    hi = x.astype(_MXU)
    r1 = x - hi.astype(_F32)
    mid = r1.astype(_MXU)
    lo = (r1 - mid.astype(_F32)).astype(_MXU)
    return hi, mid, lo


def _params(sem):
    return pltpu.CompilerParams(dimension_semantics=sem, vmem_limit_bytes=VMEM_LIMIT)


_C_Q, _C_KV, _C_RQ, _C_RK, _C_RV, _C_RG, _C_GT = 0, 512, 1280, 1792, 2304, 2816, 3328
_PROJ_COLS = 3456


def _proj_kernel(x_ref, g_ref, w_ref, cos_ref, sa_ref, sb_ref,
                 q_ref, rows_ref, win_ref, rows_t_ref, win_t_ref, kk_ref, vt_ref, gates_ref, ret_ref):
    x = x_ref[...]
    ms = jnp.mean(x * x, axis=-1, keepdims=True)
    h = (x * lax.rsqrt(ms + RMS_EPS) * g_ref[...]).astype(_MXU)
    cos, sa, sb = cos_ref[...], sa_ref[...], sb_ref[...]

    def seg(c0, n):
        return jnp.dot(h, w_ref[:, c0:c0 + n], preferred_element_type=_F32)

    def rope(p):
        return p * cos + pltpu.roll(p, LANES - 32, 1) * sa + pltpu.roll(p, 32, 1) * sb

    def slab(p, s):
        return p[:, s * LANES:(s + 1) * LANES]

    scale = HEAD_DIM ** -0.5
    p = seg(_C_Q, NSA_W)
    for s in range(4):
        q_ref[:, s * LANES:(s + 1) * LANES] = (rope(slab(p, s)) * (scale * LOG2E)).astype(q_ref.dtype)
    p = seg(_C_KV, 6 * KV_W)
    kc, vc = rope(slab(p, 0)), slab(p, 1)
    ks, vs = rope(slab(p, 2)), slab(p, 3)
    kw, vw = rope(slab(p, 4)), slab(p, 5)
    rows_ref[:, 0:128] = kc
    rows_ref[:, 128:256] = vc
    rows_ref[:, 256:384] = ks
    rows_ref[:, 384:512] = vs
    win_ref[:, 0:128] = kw
    win_ref[:, 128:256] = vw
    kk_ref[:, 0:128] = ks.astype(kk_ref.dtype)
    kk_ref[:, 128:256] = kw.astype(kk_ref.dtype)
    vs_t, vw_t = vs.T, vw.T
    rows_t_ref[0] = kc.T
    rows_t_ref[1] = vc.T
    rows_t_ref[2] = ks.T
    rows_t_ref[3] = vs_t
    win_t_ref[0] = kw.T
    win_t_ref[1] = vw_t
    ones = jnp.ones((VT_ROWS - KV_W, vs_t.shape[1]), vt_ref.dtype)
    vt_ref[0, 0:KV_W] = vs_t.astype(vt_ref.dtype)
    vt_ref[0, KV_W:VT_ROWS] = ones
    vt_ref[1, 0:KV_W] = vw_t.astype(vt_ref.dtype)
    vt_ref[1, KV_W:VT_ROWS] = ones
    p = seg(_C_RQ, RET_W)
    for s in range(4):
        ret_ref[:, s * LANES:(s + 1) * LANES] = rope(slab(p, s))
    p = seg(_C_RK, RET_W)
    for s in range(4):
        ret_ref[:, RET_W + s * LANES:RET_W + (s + 1) * LANES] = rope(slab(p, s)) * scale
    ret_ref[:, 2 * RET_W:3 * RET_W] = seg(_C_RV, RET_W)
    ret_ref[:, 3 * RET_W:4 * RET_W] = seg(_C_RG, RET_W)
    gates_ref[...] = jax.nn.sigmoid(seg(_C_GT, LANES))


def _proj(x2, ln, w_perm, cos, sa, sb, tm):
    n, d = x2.shape
    tt = cos.shape[0]
    nt = tt // tm
    row = lambda i: (i, 0)
    tab = lambda i: (i % nt, 0)
    const = lambda i: (0, 0)
    out_shape = (
        jax.ShapeDtypeStruct((n, NSA_W), _MXU),
        jax.ShapeDtypeStruct((n, 4 * KV_W), _F32),
        jax.ShapeDtypeStruct((n, 2 * KV_W), _F32),
        jax.ShapeDtypeStruct((n // tt, 4, KV_W, tt), _F32),
        jax.ShapeDtypeStruct((n // tt, 2, KV_W, tt), _F32),
        jax.ShapeDtypeStruct((n, 2 * KV_W), _MXU),
        jax.ShapeDtypeStruct((n // tt, 2, VT_ROWS, tt), _MXU),
        jax.ShapeDtypeStruct((n, LANES), _F32),
        jax.ShapeDtypeStruct((n, 4 * RET_W), _F32),
    )
    return pl.pallas_call(
        _proj_kernel,
        grid=(n // tm,),
        in_specs=[
            pl.BlockSpec((tm, d), row),
            pl.BlockSpec((1, d), const),
            pl.BlockSpec((d, _PROJ_COLS), const),
            pl.BlockSpec((tm, LANES), tab),
            pl.BlockSpec((tm, LANES), tab),
            pl.BlockSpec((tm, LANES), tab),
        ],
        out_specs=(
            pl.BlockSpec((tm, NSA_W), row),
            pl.BlockSpec((tm, 4 * KV_W), row),
            pl.BlockSpec((tm, 2 * KV_W), row),
            pl.BlockSpec((None, 4, KV_W, tm), lambda i: (i // nt, 0, 0, i % nt)),
            pl.BlockSpec((None, 2, KV_W, tm), lambda i: (i // nt, 0, 0, i % nt)),
            pl.BlockSpec((tm, 2 * KV_W), row),
            pl.BlockSpec((None, 2, VT_ROWS, tm), lambda i: (i // nt, 0, 0, i % nt)),
            pl.BlockSpec((tm, LANES), row),
            pl.BlockSpec((tm, 4 * RET_W), row),
        ),
        out_shape=out_shape,
        compiler_params=_params(("parallel",)),
        name="proj",
    )(x2, ln, w_perm, cos, sa, sb)


def _post_kernel(x_ref, on_ref, or_ref, wo_ref, g2_ref, wg_ref, wu_ref, wd_ref, gf_ref, y_ref):
    x = x_ref[...]
    mix = (jnp.dot(on_ref[...], wo_ref[0:NSA_W, :], preferred_element_type=_F32)
           + jnp.dot(or_ref[...], wo_ref[NSA_W:NSA_W + RET_W, :], preferred_element_type=_F32))
    x1 = x + mix
    ms = jnp.mean(x1 * x1, axis=-1, keepdims=True)
    h = (x1 * lax.rsqrt(ms + RMS_EPS) * g2_ref[...]).astype(_MXU)
    a = jax.nn.silu(jnp.dot(h, wg_ref[...], preferred_element_type=_F32))
    a = a * jnp.dot(h, wu_ref[...], preferred_element_type=_F32)
    y = x1 + jnp.dot(a.astype(_MXU), wd_ref[...], preferred_element_type=_F32)
    ms = jnp.mean(y * y, axis=-1, keepdims=True)
    y_ref[...] = y * lax.rsqrt(ms + RMS_EPS) * gf_ref[...]


def _post(x2, o_nsa, o_ret, w_out, ln2, w_gate, w_up, w_down, ln_f, tm):
    n, d = x2.shape
    dff = w_gate.shape[1]
    row = lambda i: (i, 0)
    const = lambda i: (0, 0)
    once = dict(pipeline_mode=pl.Buffered(1))
    return pl.pallas_call(
        _post_kernel,
        grid=(n // tm,),
        in_specs=[
            pl.BlockSpec((tm, d), row),
            pl.BlockSpec((tm, NSA_W), row),
            pl.BlockSpec((tm, RET_W), row),
            pl.BlockSpec((d, d), const, **once),
            pl.BlockSpec((1, d), const),
            pl.BlockSpec((d, dff), const, **once),
            pl.BlockSpec((d, dff), const, **once),
            pl.BlockSpec((dff, d), const, **once),
            pl.BlockSpec((1, d), const),
        ],
        out_specs=pl.BlockSpec((tm, d), row),
        out_shape=jax.ShapeDtypeStruct((n, d), _F32),
        compiler_params=_params(("parallel",)),
        name="post",
    )(x2, o_nsa, o_ret, w_out, ln2, w_gate, w_up, w_down, ln_f)


def _compress_rows(planes, buf_ref, c, nb16, pos_ref, w1_ref, w2_ref):
    for j, plane in enumerate(planes):
        buf_ref[c, j * LANES:(j + 1) * LANES, :] = plane.T
    half = CMP_STRIDE // 2
    a = jnp.zeros((nb16, 2 * CMP_HID), _F32)
    b = jnp.zeros((nb16, 2 * CMP_HID), _F32)
    for tp in range(half):
        t0, t1 = 2 * tp, 2 * tp + 1
        x0 = buf_ref[c, pl.ds(t0, nb16, stride=CMP_STRIDE), :]
        x1 = buf_ref[c, pl.ds(t1, nb16, stride=CMP_STRIDE), :]
        xa = jnp.concatenate([x0 + pos_ref[c, t0:t0 + 1, :], x1 + pos_ref[c, t1:t1 + 1, :]], axis=1)
        xb = jnp.concatenate([x0 + pos_ref[c, CMP_STRIDE + t0:CMP_STRIDE + t0 + 1, :],
                              x1 + pos_ref[c, CMP_STRIDE + t1:CMP_STRIDE + t1 + 1, :]], axis=1)
        a = a + _dot(xa, w1_ref[c, tp])
        b = b + _dot(xb, w1_ref[c, half + tp])
    hid = a + pltpu.roll(b, nb16 - 1, 0)
    return _dot(jax.nn.gelu(hid), w2_ref[c])


def _compress_prompt_kernel(rt_ref, pos_ref, w1_ref, w2_ref, kcb_ref, vcbt_ref, buf_ref):
    nb16 = kcb_ref.shape[0]
    outs = []
    for c in range(2):
        planes = [rt_ref[c, :, j * LANES:(j + 1) * LANES] for j in range(nb16 * CMP_STRIDE // LANES)]
        outs.append(_compress_rows(planes, buf_ref, c, nb16, pos_ref, w1_ref, w2_ref))
    kcb_ref[...] = outs[0]
    vcbt_ref[...] = outs[1].T


def _compress_prompt(rows_t, pos, w1bd, w2bd):
    b, _, _, t = rows_t.shape
    nb16 = t // CMP_STRIDE
    c3 = lambda i: (0, 0, 0)
    c4 = lambda i: (0, 0, 0, 0)
    out = jax.ShapeDtypeStruct((b, nb16, LANES), _F32)
    return pl.pallas_call(
        _compress_prompt_kernel,
        grid=(b,),
        in_specs=[
            pl.BlockSpec((None, 2, KV_W, t), lambda i: (i, 0, 0, 0)),
            pl.BlockSpec(pos.shape, c3),
            pl.BlockSpec(w1bd.shape, c4),
            pl.BlockSpec(w2bd.shape, c3),
        ],
        out_specs=(pl.BlockSpec((None, nb16, LANES), lambda i: (i, 0, 0)),
                   pl.BlockSpec((None, LANES, nb16), lambda i: (i, 0, 0))),
        out_shape=(out, jax.ShapeDtypeStruct((b, LANES, nb16), _F32)),
        scratch_shapes=[pltpu.VMEM((2, t, LANES), _F32)],
        compiler_params=_params(("parallel",)),
        name="compress_prompt",
    )(rows_t, pos, w1bd, w2bd)


def _compress_sample_kernel(n_pages, pt_ref, *refs):
    page_refs = refs[:n_pages]
    pos_ref, w1_ref, w2_ref, kcb_ref, vcb_ref, buf_ref = refs[n_pages:]
    nb16 = kcb_ref.shape[0]
    for c, out_ref in enumerate((kcb_ref, vcb_ref)):
        planes = [page_refs[j][c] for j in range(n_pages)]
        out_ref[...] = _compress_rows(planes, buf_ref, c, nb16, pos_ref, w1_ref, w2_ref)


def _compress_sample(pool_t, page_table_flat, b2, n_pages, pos, w1bd, w2bd):
    past = n_pages * PAGE_SIZE
    nb16 = past // CMP_STRIDE
    c3 = lambda i, pt: (0, 0, 0)
    c4 = lambda i, pt: (0, 0, 0, 0)

    def page_spec(j):
        return pl.BlockSpec((None, 2, KV_W, PAGE_SIZE), lambda i, pt: (pt[i * n_pages + j], 0, 0, 0))

    out = jax.ShapeDtypeStruct((b2, nb16, LANES), _F32)
    grid_spec = pltpu.PrefetchScalarGridSpec(
        num_scalar_prefetch=1,
        grid=(b2,),
        in_specs=[page_spec(j) for j in range(n_pages)] + [
            pl.BlockSpec(pos.shape, c3),
            pl.BlockSpec(w1bd.shape, c4),
            pl.BlockSpec(w2bd.shape, c3),
        ],
        out_specs=(pl.BlockSpec((None, nb16, LANES), lambda i, pt: (i, 0, 0)),
                   pl.BlockSpec((None, nb16, LANES), lambda i, pt: (i, 0, 0))),
        scratch_shapes=[pltpu.VMEM((2, past, LANES), _F32)],
    )
    return pl.pallas_call(
        functools.partial(_compress_sample_kernel, n_pages),
        grid_spec=grid_spec,
        out_shape=(out, out),
        compiler_params=_params(("arbitrary",)),
        name="compress_sample",
    )(page_table_flat, *([pool_t] * n_pages), pos, w1bd, w2bd)


def _group_queries(qf, rows):
    lane = lax.broadcasted_iota(jnp.int32, (rows, LANES), 1)
    out = []
    for h in range(NSA_HEADS):
        g = h // HPG
        s = qf[:, (h // 2) * LANES:(h // 2 + 1) * LANES]
        if (h % 2) != g:
            s = pltpu.roll(s, HEAD_DIM, 1)
        out.append(jnp.where((lane >= g * HEAD_DIM) & (lane < (g + 1) * HEAD_DIM), s, 0.0))
    return out


def _masked_softmax(s, mask):
    s = jnp.where(mask, s, -jnp.inf)
    m = jnp.max(s, axis=-1, keepdims=True)
    m = jnp.where(m > -jnp.inf, m, 0.0)
    e = jnp.where(mask, jnp.exp2(s - m), 0.0)
    return e / jnp.maximum(jnp.sum(e, axis=-1, keepdims=True), 1e-30)


def _place_heads(o_heads, rows):
    lane = lax.broadcasted_iota(jnp.int32, (rows, LANES), 1)
    slabs = []
    for k in range(NSA_HEADS // 2):
        pair = []
        for h in (2 * k, 2 * k + 1):
            g = h // HPG
            o = o_heads[h]
            if (h % 2) != g:
                o = pltpu.roll(o, HEAD_DIM, 1)
            pair.append(o)
        slabs.append(jnp.where(lane < HEAD_DIM, pair[0], pair[1]))
    return slabs


def _softmax_keys(s):
    m = jnp.max(s, axis=0, keepdims=True)
    m = jnp.where(m > -jnp.inf, m, 0.0)
    e = jnp.exp2(s - m)
    return e * (1.0 / jnp.maximum(jnp.sum(e, axis=0, keepdims=True), 1e-30))


def _weights_keys(s):
    m = jnp.max(s, axis=0, keepdims=True)
    m = jnp.where(m > -jnp.inf, m, 0.0)
    return jnp.exp2((s - m).astype(_MXU))


def _normalise(o_aug):
    return o_aug[0:KV_W] * (1.0 / jnp.maximum(o_aug[KV_W:KV_W + 1], 1e-30))


def _nsa_prompt_kernel(nsel, q_ref, gates_ref, kcb_ref, vcbt_ref, kk_ref, vt_ref, eoh_ref, cover_ref,
                       o_ref, acc_ref, m_ref, s_ref, qa_ref):
    i = pl.program_id(1)
    nseq = q_ref.shape[0]
    seqs = range(nseq)
    tq = Q_BLK
    qpos_1 = i * tq + lax.broadcasted_iota(jnp.int32, (1, tq), 1)

    def per_head(x):
        return jnp.concatenate([x] * NSA_HEADS, axis=1)

    nb16 = kcb_ref.shape[1]
    blk_end = lax.broadcasted_iota(jnp.int32, (nb16, 1), 0) * CMP_STRIDE + (CMP_BLK - 1)
    bias_c = per_head(jnp.where(blk_end <= qpos_1, 0.0, -jnp.inf))
    q2 = [jnp.concatenate(_group_queries(q_ref[sq].astype(_F32), tq), axis=0).astype(_MXU)
          for sq in seqs]
    p_c = [_softmax_keys(_dot_nt(kcb_ref[sq], q2[sq]) + bias_c) for sq in seqs]
    o_c = [_dot(vcbt_ref[sq], p_c[sq]) for sq in seqs]

    jidx = lax.broadcasted_iota(jnp.int32, (nsel, tq), 0)
    qpos_l = i * tq + lax.broadcasted_iota(jnp.int32, (nsel, tq), 1)
    cur = qpos_l // SEL_BLK
    forced = (jidx == 0) | (jidx == cur) | (jidx == cur - 1)
    valid = jidx * SEL_BLK <= qpos_l
    n_top = min(SEL_TOPN, nsel)
    sub = lax.broadcasted_iota(jnp.int32, (8, tq), 0)
    nv = nsel // 8

    def choice_bias(p_cs, g):
        psum = p_cs[:, (g * HPG) * tq:(g * HPG + 1) * tq]
        for hh in range(1, HPG):
            psum = psum + p_cs[:, (g * HPG + hh) * tq:(g * HPG + hh + 1) * tq]
        imp_t = jnp.zeros((nsel, tq), _F32)
        for part in _split3(psum):
            imp_t = imp_t + jnp.dot(cover_ref[...], part, preferred_element_type=_F32)
        score = jnp.where(forced, jnp.inf, jnp.where(valid, imp_t, -jnp.inf))
        sc_v = [score[8 * v:8 * v + 8] for v in range(nv)]
        rank_v = [jnp.zeros((8, tq), jnp.int32) for _ in range(nv)]
        for jp in range(nsel):
            rowb = jnp.broadcast_to(score[jp:jp + 1, :], (8, tq))
            for v in range(nv):
                if v > jp // 8:
                    beats = jnp.where(rowb >= sc_v[v], 1, 0)
                elif v < jp // 8:
                    beats = jnp.where(rowb > sc_v[v], 1, 0)
                else:
                    beats = jnp.where(sub > (jp % 8), jnp.where(rowb >= sc_v[v], 1, 0),
                                      jnp.where(rowb > sc_v[v], 1, 0))
                rank_v[v] = rank_v[v] + beats
        sel_t = jnp.concatenate([jnp.where(r < n_top, 1.0, 0.0) for r in rank_v], axis=0)
        if nsel < LANES:
            sel_t = jnp.concatenate([sel_t, jnp.ones((LANES - nsel, tq), _F32)], axis=0)
        return ((sel_t.T - 1.0) * (-NEG_BIG)).astype(_MXU)

    for sq in seqs:
        qa_ref[sq, :, 0:LANES] = q2[sq]
        for g in range(KV_GROUPS):
            bias = choice_bias(p_c[sq], g)
            for hh in range(HPG):
                h = g * HPG + hh
                qa_ref[sq, h * tq:(h + 1) * tq, LANES:2 * LANES] = bias

    tk = 2 * tq
    acc_ref[...] = jnp.zeros(acc_ref.shape, _F32)
    m_ref[...] = jnp.full(m_ref.shape, 2.0 * NEG_BIG, _F32)

    def scores(slot, kt):
        k0 = pl.multiple_of(kt * tk, tk)
        eoh = eoh_ref[pl.ds(k0, tk), :]
        for sq in seqs:
            k_aug = jnp.concatenate([kk_ref[sq, pl.ds(k0, tk), 0:LANES], eoh], axis=1)
            s_ref[sq, slot] = _dot_nt(k_aug, qa_ref[sq])

    def consume(slot, kt, causal):
        k0 = pl.multiple_of(kt * tk, tk)
        if causal:
            kpos = k0 + lax.broadcasted_iota(jnp.int32, (tk, 1), 0)
            cbias = per_head(jnp.where(kpos <= qpos_1, 0.0, NEG_BIG))
        for sq in seqs:
            s = s_ref[sq, slot]
            if causal:
                s = s + cbias
            m_old = m_ref[sq, slot]
            m_new = jnp.maximum(m_old, jnp.max(s, axis=0, keepdims=True))
            alpha = jnp.exp2(m_old - m_new)
            p = jnp.exp2((s - m_new).astype(_MXU))
            m_ref[sq, slot] = m_new
            acc_ref[sq, slot] = alpha * acc_ref[sq, slot] + _dot(vt_ref[sq, 0, :, pl.ds(k0, tk)], p)

    last = (i + 2) // 2 - 1
    scores(0, 0)

    def tile_pair(j, carry):
        scores(1, 2 * j + 1)
        consume(0, 2 * j, False)
        scores(0, 2 * j + 2)
        consume(1, 2 * j + 1, False)
        return carry

    lax.fori_loop(0, last // 2, tile_pair, 0)

    @pl.when(last % 2 == 1)
    def _():
        scores(1, last)
        consume(0, last - 1, False)
        consume(1, last, True)

    @pl.when(last % 2 == 0)
    def _():
        consume(0, last, True)

    wk = WINDOW + tq
    start = pl.multiple_of(jnp.maximum(i * tq - WINDOW, 0), tq)
    wpos = start + lax.broadcasted_iota(jnp.int32, (wk, 1), 0)
    bias_w = per_head(jnp.where(wpos <= qpos_1, jnp.where(wpos > qpos_1 - WINDOW, 0.0, -jnp.inf), -jnp.inf))
    for sq in seqs:
        m_all = jnp.maximum(m_ref[sq, 0], m_ref[sq, 1])
        o_s = _normalise(jnp.exp2(m_ref[sq, 0] - m_all) * acc_ref[sq, 0]
                         + jnp.exp2(m_ref[sq, 1] - m_all) * acc_ref[sq, 1])
        e_w = _weights_keys(_dot_nt(kk_ref[sq, pl.ds(start, wk), LANES:2 * LANES], q2[sq]) + bias_w)
        o_w = _normalise(_dot(vt_ref[sq, 1, :, pl.ds(start, wk)], e_w))

        gt_t = gates_ref[sq].T
        o_rows = []
        for h in range(NSA_HEADS):
            g = h // HPG
            rs = slice(g * HEAD_DIM, (g + 1) * HEAD_DIM)
            cs = slice(h * tq, (h + 1) * tq)
            o_rows.append(gt_t[3 * h:3 * h + 1] * o_c[sq][rs, cs] + gt_t[3 * h + 1:3 * h + 2] * o_s[rs, cs]
                          + gt_t[3 * h + 2:3 * h + 3] * o_w[rs, cs])
        o_ref[sq] = jnp.concatenate(o_rows, axis=0).T.astype(o_ref.dtype)


def _nsa_prompt(q3, gates3, kcb, vcbt, kk3, vt4, eoh, cover, nseq):
    b, t, _ = q3.shape
    nq = t // Q_BLK
    nb16 = t // CMP_STRIDE
    nsel = t // SEL_BLK
    qblk = lambda bi, i: (bi, i, 0)
    per_b = lambda bi, i: (bi, 0, 0)
    const = lambda bi, i: (0, 0)
    ncol = NSA_HEADS * Q_BLK
    return pl.pallas_call(
        functools.partial(_nsa_prompt_kernel, nsel),
        grid=(b // nseq, nq),
        in_specs=[
            pl.BlockSpec((nseq, Q_BLK, NSA_W), qblk),
            pl.BlockSpec((nseq, Q_BLK, LANES), qblk),
            pl.BlockSpec((nseq, nb16, LANES), per_b),
            pl.BlockSpec((nseq, LANES, nb16), per_b),
            pl.BlockSpec((nseq, t, 2 * KV_W), per_b),
            pl.BlockSpec((nseq, 2, VT_ROWS, t), lambda bi, i: (bi, 0, 0, 0)),
            pl.BlockSpec((t, LANES), const),
            pl.BlockSpec((nsel, nb16), const),
        ],
        out_specs=pl.BlockSpec((nseq, Q_BLK, NSA_W), qblk),
        out_shape=jax.ShapeDtypeStruct((b, t, NSA_W), _MXU),
        scratch_shapes=[pltpu.VMEM((nseq, 2, VT_ROWS, ncol), _F32),
                        pltpu.VMEM((nseq, 2, 1, ncol), _F32),
                        pltpu.VMEM((nseq, 2, 2 * Q_BLK, ncol), _F32),
                        pltpu.VMEM((nseq, ncol, 2 * LANES), _MXU)],
        compiler_params=_params(("parallel", "arbitrary")),
        name="nsa_prompt",
    )(q3, gates3, kcb, vcbt, kk3, vt4, eoh, cover)


def _ret_prompt_kernel(ret_ref, dmat_ref, rowdec_ref, kdec_ref, sdec_ref, bmask_ref,
                       o_ref, st_ref, s_ref):
    c = pl.program_id(1)
    nc = pl.num_programs(1)
    nseq = ret_ref.shape[0]
    tq = RET_CHUNK

    @pl.when(c == 0)
    def _():
        s_ref[...] = jnp.zeros(s_ref.shape, _F32)

    lane = lax.broadcasted_iota(jnp.int32, (tq, LANES), 1)
    lo = lane < HEAD_DIM
    bmask = bmask_ref[...]
    for pr in range(RET_HEADS // 2):
        c0 = pr * LANES
        dm = jnp.concatenate([dmat_ref[2 * pr], dmat_ref[2 * pr + 1]], axis=0)
        for sq in range(nseq):
            q = ret_ref[sq, :, c0:c0 + LANES]
            k = ret_ref[sq, :, RET_W + c0:RET_W + c0 + LANES]
            v = ret_ref[sq, :, 2 * RET_W + c0:2 * RET_W + c0 + LANES]
            g = ret_ref[sq, :, 3 * RET_W + c0:3 * RET_W + c0 + LANES]
            q2 = jnp.concatenate([jnp.where(lo, q, 0.0), jnp.where(lo, 0.0, q)], axis=0)
            o2 = _dot(_dot_nt(q2, k) * dm, v)
            o = jnp.where(lo, o2[0:tq], o2[tq:2 * tq])
            s_old = s_ref[sq, pr]
            o = o + _dot(q, s_old) * rowdec_ref[pr]
            s_ref[sq, pr] = s_old * sdec_ref[pr] + _dot_tn(k * kdec_ref[pr], v) * bmask
            o_sq = o * o
            s0 = jnp.sum(jnp.where(lo, o_sq, 0.0), axis=-1, keepdims=True)
            s1 = jnp.sum(jnp.where(lo, 0.0, o_sq), axis=-1, keepdims=True)
            ms = jnp.where(lo, s0, s1) * (1.0 / HEAD_DIM)
            o = o * lax.rsqrt(ms + RMS_EPS) * jax.nn.silu(g)
            o_ref[sq, :, c0:c0 + LANES] = o.astype(o_ref.dtype)

    @pl.when(c == nc - 1)
    def _():
        for sq in range(nseq):
            for pr in range(RET_HEADS // 2):
                s_fin = s_ref[sq, pr]
                st_ref[sq, 2 * pr] = s_fin[0:HEAD_DIM, 0:HEAD_DIM]
                st_ref[sq, 2 * pr + 1] = pltpu.roll(s_fin, HEAD_DIM, 1)[HEAD_DIM:2 * HEAD_DIM, 0:HEAD_DIM]


def _ret_prompt(ret3, tabs, nseq):
    b, t, _ = ret3.shape
    nc = t // RET_CHUNK
    blk = lambda bi, i: (bi, i, 0)
    c3 = lambda bi, i: (0, 0, 0)
    dmat, rowdec, kdec, sdec, bmask = tabs
    return pl.pallas_call(
        _ret_prompt_kernel,
        grid=(b // nseq, nc),
        in_specs=[
            pl.BlockSpec((nseq, RET_CHUNK, 4 * RET_W), blk),
            pl.BlockSpec(dmat.shape, c3),
            pl.BlockSpec(rowdec.shape, c3),
            pl.BlockSpec(kdec.shape, c3),
            pl.BlockSpec(sdec.shape, c3),
            pl.BlockSpec(bmask.shape, lambda bi, i: (0, 0)),
        ],
        out_specs=(pl.BlockSpec((nseq, RET_CHUNK, RET_W), blk),
                   pl.BlockSpec((nseq, RET_HEADS, HEAD_DIM, HEAD_DIM), lambda bi, i: (bi, 0, 0, 0))),
        out_shape=(jax.ShapeDtypeStruct((b, t, RET_W), _MXU),
                   jax.ShapeDtypeStruct((b, RET_HEADS, HEAD_DIM, HEAD_DIM), _F32)),
        scratch_shapes=[pltpu.VMEM((nseq, RET_HEADS // 2, LANES, LANES), _F32)],
        compiler_params=_params(("parallel", "arbitrary")),
        name="ret_prompt",
    )(ret3, dmat, rowdec, kdec, sdec, bmask)


def _sample_queries(q_row):
    heads = _group_queries(q_row.astype(_F32), 1)
    row = lax.broadcasted_iota(jnp.int32, (NSA_HEADS, LANES), 0)
    q8 = jnp.zeros((NSA_HEADS, LANES), _F32)
    for h in range(NSA_HEADS):
        q8 = jnp.where(row == h, jnp.broadcast_to(heads[h], (NSA_HEADS, LANES)), q8)
    return q8


def _s_topk_kernel(q_pos, nsel, q_ref, kcb_ref, vcb_ref, cover_ref, idx_ref, oc_ref):
    nseq, nb16, _ = kcb_ref.shape
    nrow = nseq * KV_GROUPS
    blk_end = lax.broadcasted_iota(jnp.int32, (NSA_HEADS, nb16), 1) * CMP_STRIDE + (CMP_BLK - 1)
    row = lax.broadcasted_iota(jnp.int32, (nrow, nb16), 0)
    psum = jnp.zeros((nrow, nb16), _F32)
    for s in range(nseq):
        q8 = _sample_queries(q_ref[s])
        p_c = _masked_softmax(_dot_nt(q8, kcb_ref[s]), blk_end <= q_pos)
        oc_ref[s] = _dot(p_c, vcb_ref[s])
        for g in range(KV_GROUPS):
            acc = p_c[g * HPG:g * HPG + 1]
            for hh in range(1, HPG):
                acc = acc + p_c[g * HPG + hh:g * HPG + hh + 1]
            psum = jnp.where(row == KV_GROUPS * s + g, jnp.broadcast_to(acc, (nrow, nb16)), psum)

    npad = cover_ref.shape[0]
    imp = jnp.zeros((nrow, npad), _F32)
    for part in _split3(psum):
        imp = imp + lax.dot_general(part, cover_ref[...], (((1,), (1,)), ((), ())),
                                    preferred_element_type=_F32)
    j = lax.broadcasted_iota(jnp.int32, (nrow, npad), 1)
    cur = q_pos // SEL_BLK
    forced = (j == 0) | (j == cur) | (j == cur - 1)
    valid = j * SEL_BLK <= q_pos
    score = jnp.where(forced, jnp.inf, jnp.where(valid, imp, -jnp.inf))
    jf = j.astype(_F32)
    alive = jnp.where(j < nsel, 1.0, 0.0)
    lane = lax.broadcasted_iota(jnp.int32, (nrow, LANES), 1)
    idx = jnp.zeros((nrow, LANES), _F32)
    for r in range(min(SEL_TOPN, nsel)):
        live = alive > 0.0
        m = jnp.max(jnp.where(live, score, -jnp.inf), axis=-1, keepdims=True)
        cand = jnp.where(live, jnp.where(score == m, jf, float(npad)), float(npad))
        jmin = jnp.min(cand, axis=-1, keepdims=True)
        idx = jnp.where(lane == r, jmin, idx)
        alive = jnp.where(jf == jmin, 0.0, alive)
    idx_ref[...] = idx.astype(jnp.int32)


def _s_topk(q3, kcb, vcb, cover_s, q_pos, nsel):
    b2 = q3.shape[0]
    nb16 = kcb.shape[1]
    nseq = 8 if b2 % 8 == 0 else b2
    per_b = lambda i: (i, 0, 0)
    return pl.pallas_call(
        functools.partial(_s_topk_kernel, q_pos, nsel),
        grid=(b2 // nseq,),
        in_specs=[
            pl.BlockSpec((nseq, 1, NSA_W), per_b),
            pl.BlockSpec((nseq, nb16, LANES), per_b),
            pl.BlockSpec((nseq, nb16, LANES), per_b),
            pl.BlockSpec(cover_s.shape, lambda i: (0, 0)),
        ],
        out_specs=(pl.BlockSpec((nseq * KV_GROUPS, LANES), lambda i: (i, 0)),
                   pl.BlockSpec((nseq, NSA_HEADS, LANES), per_b)),
        out_shape=(jax.ShapeDtypeStruct((b2 * KV_GROUPS, LANES), jnp.int32),
                   jax.ShapeDtypeStruct((b2, NSA_HEADS, LANES), _F32)),
        compiler_params=_params(("parallel",)),
        name="sample_topk",
    )(q3, kcb, vcb, cover_s)


def _s_attn_kernel(q_pos, nsel, past, wbuf, idx_ref, pt_ref, *refs):
    del pt_ref
    n_top = min(SEL_TOPN, nsel)
    nblk = KV_GROUPS * n_top
    blk_refs = refs[:nblk]
    q_ref, gates_ref, oc_ref, rows_ref, win_ref, cw_ref, o_ref = refs[nblk:]
    b = pl.program_id(0)
    q8 = _sample_queries(q_ref[...])
    row1 = lax.broadcasted_iota(jnp.int32, (NSA_HEADS, 1), 0)
    is_g0 = row1 < HPG

    nk = n_top * PAGE_SIZE
    lane_k = lax.broadcasted_iota(jnp.int32, (1, nk), 1)
    kslot = lane_k // PAGE_SIZE
    khalf = (lane_k % PAGE_SIZE) // SEL_BLK
    o_s = None
    for g in range(KV_GROUPS):
        kcat = jnp.concatenate([blk_refs[g * n_top + k][0] for k in range(n_top)], axis=1)
        vcat = jnp.concatenate([blk_refs[g * n_top + k][1] for k in range(n_top)], axis=1)
        s = _dot(q8, kcat)
        bias = jnp.full((1, nk), NEG_BIG, _F32)
        has_new = jnp.zeros((1, 1), _F32)
        for k in range(n_top):
            jk = idx_ref[(b * KV_GROUPS + g) * n_top + k]
            is_new = jk == nsel - 1
            half = jnp.where(is_new, -1, jk % 2)
            bias = jnp.where((kslot == k) & (khalf == half), 0.0, bias)
            has_new = jnp.where(is_new, 1.0, has_new)
        s = s + bias
        k_new = rows_ref[:, 2 * LANES:3 * LANES]
        v_new = rows_ref[:, 3 * LANES:4 * LANES]
        s_new = jnp.sum(q8 * k_new, axis=-1, keepdims=True) + jnp.where(has_new > 0.0, 0.0, NEG_BIG)
        m = jnp.maximum(jnp.max(s, axis=-1, keepdims=True), s_new)
        e = jnp.exp2(s - m)
        e_new = jnp.exp2(s_new - m)
        den = jnp.sum(e, axis=-1, keepdims=True) + e_new
        o_g = (_dot_nt(e, vcat) + e_new * v_new) / den
        o_s = o_g if o_s is None else jnp.where(is_g0, o_s, o_g)

    s_w = _dot(q8, cw_ref[0])
    wpos = (past - wbuf) + lax.broadcasted_iota(jnp.int32, (1, wbuf), 1)
    wmask = (wpos <= q_pos) & (wpos > q_pos - WINDOW) & (wpos >= 0)
    s_w = jnp.where(wmask, s_w, NEG_BIG)
    s_wn = jnp.sum(q8 * win_ref[:, 0:LANES], axis=-1, keepdims=True)
    m = jnp.maximum(jnp.max(s_w, axis=-1, keepdims=True), s_wn)
    e = jnp.exp2(s_w - m)
    e_new = jnp.exp2(s_wn - m)
    den = jnp.sum(e, axis=-1, keepdims=True) + e_new
    o_w = (_dot_nt(e, cw_ref[1]) + e_new * win_ref[:, LANES:2 * LANES]) / den

    gt = jnp.broadcast_to(gates_ref[...], (NSA_HEADS, LANES))
    lane = lax.broadcasted_iota(jnp.int32, (NSA_HEADS, LANES), 1)
    row = lax.broadcasted_iota(jnp.int32, (NSA_HEADS, LANES), 0)
    o = jnp.zeros((NSA_HEADS, LANES), _F32)
    for jb, ob in enumerate((oc_ref[...], o_s, o_w)):
        gcol = jnp.sum(jnp.where(lane == 3 * row + jb, gt, 0.0), axis=-1, keepdims=True)
        o = o + gcol * ob
    heads = [o[h:h + 1] for h in range(NSA_HEADS)]
    for k, s in enumerate(_place_heads(heads, 1)):
        o_ref[:, k * LANES:(k + 1) * LANES] = s.astype(o_ref.dtype)


def _s_attn(idx_flat, pt_flat, pool_t, q3, gates3, oc, rows3, win3, cache_win_t, q_pos, nsel, past, n_pages):
    b2 = q3.shape[0]
    wbuf = cache_win_t.shape[3]
    n_top = min(SEL_TOPN, nsel)
    per_b = lambda i, idx, pt: (i, 0, 0)

    def blk_spec(g, k):
        def imap(i, idx, pt):
            j = jnp.minimum(idx[(i * KV_GROUPS + g) * n_top + k], nsel - 2)
            return (pt[i * n_pages + j // 2], 1, 0, 0)
        return pl.BlockSpec((None, 2, KV_W, PAGE_SIZE), imap)

    grid_spec = pltpu.PrefetchScalarGridSpec(
        num_scalar_prefetch=2,
        grid=(b2,),
        in_specs=[blk_spec(g, k) for g in range(KV_GROUPS) for k in range(n_top)] + [
            pl.BlockSpec((None, 1, NSA_W), per_b),
            pl.BlockSpec((None, 1, LANES), per_b),
            pl.BlockSpec((None, NSA_HEADS, LANES), per_b),
            pl.BlockSpec((None, 1, 4 * KV_W), per_b),
            pl.BlockSpec((None, 1, 2 * KV_W), per_b),
            pl.BlockSpec((None, 2, KV_W, wbuf), lambda i, idx, pt: (i, 0, 0, 0)),
        ],
        out_specs=pl.BlockSpec((None, 1, NSA_W), per_b),
    )
    return pl.pallas_call(
        functools.partial(_s_attn_kernel, q_pos, nsel, past, wbuf),
        grid_spec=grid_spec,
        out_shape=jax.ShapeDtypeStruct((b2, 1, NSA_W), _F32),
        compiler_params=_params(("arbitrary",)),
        name="sample_attn",
    )(idx_flat, pt_flat, *([pool_t] * (KV_GROUPS * n_top)), q3, gates3, oc, rows3, win3, cache_win_t)


def _s_ret_kernel(q_ref, k_ref, v_ref, g_ref, gam_ref, st_ref, e1_ref, e2_ref, o_ref, sn_ref):
    q, k, v = q_ref[...], k_ref[...], v_ref[...]
    gam = gam_ref[...]
    st = st_ref[...]
    q_exp = _dot(q, e1_ref[...])
    k_exp = _dot(k, e1_ref[...])
    v_exp = _dot(v, e2_ref[...])
    prod = q_exp * st.astype(_MXU).astype(_F32)
    qs = jnp.zeros(q.shape, _F32)
    for part in _split3(prod):
        qs = qs + lax.dot_general(part, e2_ref[...], (((1,), (1,)), ((), ())), preferred_element_type=_F32)
    qr = q.astype(_MXU).astype(_F32)
    kr = k.astype(_MXU).astype(_F32)
    att = jnp.sum(qr * kr, axis=-1, keepdims=True)
    o = att.astype(_MXU).astype(_F32) * v.astype(_MXU).astype(_F32) + qs * gam
    sn_ref[...] = st * gam + k_exp * v_exp
    ms = jnp.mean(o * o, axis=-1, keepdims=True)
    o_ref[...] = (o * lax.rsqrt(ms + RMS_EPS) * jax.nn.silu(g_ref[...])).astype(o_ref.dtype)


def _s_ret(q, k, v, g, gam, st2, e1, e2, tr):
    n = q.shape[0]
    dd = HEAD_DIM * HEAD_DIM
    row = lambda i: (i, 0)
    const = lambda i: (0, 0)
    return pl.pallas_call(
        _s_ret_kernel,
        grid=(n // tr,),
        in_specs=[
            pl.BlockSpec((tr, HEAD_DIM), row),
            pl.BlockSpec((tr, HEAD_DIM), row),
            pl.BlockSpec((tr, HEAD_DIM), row),
            pl.BlockSpec((tr, HEAD_DIM), row),
            pl.BlockSpec((tr, 1), row),
            pl.BlockSpec((tr, dd), row),
            pl.BlockSpec((HEAD_DIM, dd), const),
            pl.BlockSpec((HEAD_DIM, dd), const),
        ],
        out_specs=(pl.BlockSpec((tr, HEAD_DIM), row), pl.BlockSpec((tr, dd), row)),
        out_shape=(jax.ShapeDtypeStruct((n, HEAD_DIM), _MXU), jax.ShapeDtypeStruct((n, dd), _F32)),
        compiler_params=_params(("parallel",)),
        name="sample_ret",
    )(q, k, v, g, gam, st2, e1, e2)


def _rope_tables(pos):
    half = HEAD_DIM // 2
    inv = 1.0 / (ROPE_THETA ** (jnp.arange(half, dtype=_F32) / half))
    ang = pos.astype(_F32)[:, None] * inv[None, :]
    cos, sin = jnp.cos(ang), jnp.sin(ang)
    zero = jnp.zeros_like(sin)
    reps = LANES // HEAD_DIM
    cos_t = jnp.tile(cos, (1, 2 * reps))
    sa = jnp.tile(jnp.concatenate([-sin, zero], axis=1), (1, reps))
    sb = jnp.tile(jnp.concatenate([zero, sin], axis=1), (1, reps))
    return cos_t, sa, sb


def _cover_matrix(nsel, nb16, nsel_pad):
    c0 = np.arange(nb16) * CMP_STRIDE
    s0 = np.arange(nsel_pad) * SEL_BLK
    m = (c0[None, :] < s0[:, None] + SEL_BLK) & (c0[None, :] + CMP_BLK > s0[:, None])
    m = m & (np.arange(nsel_pad)[:, None] < nsel)
    return jnp.asarray(m.astype(np.float32), dtype=_MXU)


def _retention_tables():
    lg = jnp.log(1.0 - 2.0 ** (-5.0 - jnp.arange(RET_HEADS, dtype=_F32)))
    c = RET_CHUNK
    i = jnp.arange(c, dtype=_F32)
    diff = i[:, None] - i[None, :]
    causal = diff >= 0
    dmat = jnp.where(causal[None], jnp.exp(jnp.where(causal, diff, 0.0)[None] * lg[:, None, None]), 0.0)
    lane_head = jnp.arange(LANES) // HEAD_DIM
    pair_lg = lg.reshape(RET_HEADS // 2, 2)[:, lane_head]
    rowdec = jnp.exp((i + 1.0)[None, :, None] * pair_lg[:, None, :])
    kdec = jnp.exp((c - 1.0 - i)[None, :, None] * pair_lg[:, None, :])
    sdec = jnp.broadcast_to(jnp.exp(c * pair_lg)[:, :, None], (RET_HEADS // 2, LANES, LANES))
    bmask = (lane_head[:, None] == lane_head[None, :]).astype(_F32)
    return lg, (dmat, rowdec, kdec, sdec, bmask)


def _compress_weights(pos, w1, w2):
    w1t = w1.reshape(CMP_BLK, HEAD_DIM, CMP_HID)
    z1 = jnp.zeros_like(w1t)
    w1bd = jnp.concatenate([jnp.concatenate([w1t, z1], axis=2), jnp.concatenate([z1, w1t], axis=2)], axis=1)
    z2 = jnp.zeros_like(w2)
    w2bd = jnp.concatenate([jnp.concatenate([w2, z2], axis=1), jnp.concatenate([z2, w2], axis=1)], axis=0)
    w1pair = w1bd.reshape(CMP_BLK // 2, 2 * LANES, 2 * CMP_HID)
    return jnp.tile(pos, (1, 2)), w1pair.astype(_MXU), w2bd.astype(_MXU)


def kernel(x_prompt, x_sample, cache_kv, cache_win, state_ret, page_table, ln1, w_in, cmp_pos_k, cmp_w1_k,
           cmp_w2_k, cmp_pos_v, cmp_w1_v, cmp_w2_v, w_out, ln2, w_gate, w_up, w_down, ln_f):
    depth = ln1.shape[0]
    assert depth == 1, "single-layer step"
    b, t, d = x_prompt.shape
    b2, s_s, _ = x_sample.shape
    assert s_s == 1
    n_pages = page_table.shape[1]
    past = n_pages * PAGE_SIZE
    wbuf = cache_win.shape[2]
    assert t % (2 * Q_BLK) == 0 and t >= WINDOW + Q_BLK and (t // SEL_BLK) % 8 == 0
    l = 0

    w = w_in[l]
    o_kv = NSA_W
    o_gt = o_kv + 6 * KV_W
    o_r = o_gt + NSA_HEADS * 3
    w_perm = jnp.concatenate([
        w[:, :o_gt], w[:, o_r:], w[:, o_gt:o_r],
        jnp.zeros((d, LANES - NSA_HEADS * 3), w.dtype)], axis=1).astype(_MXU)
    pos_k, w1k, w2k = _compress_weights(cmp_pos_k[l], cmp_w1_k[l], cmp_w2_k[l])
    pos_v, w1v, w2v = _compress_weights(cmp_pos_v[l], cmp_w1_v[l], cmp_w2_v[l])
    cpos = jnp.stack([pos_k, pos_v])
    cw1 = jnp.stack([w1k, w1v])
    cw2 = jnp.stack([w2k, w2v])
    wo, wg, wu, wd = (a[l].astype(_MXU) for a in (w_out, w_gate, w_up, w_down))
    ln1r, ln2r, lnfr = ln1[l][None, :], ln2[l][None, :], ln_f[None, :]
    lg, ret_tabs = _retention_tables()

    tm = 512 if t % 512 == 0 else 256
    xp2 = x_prompt.reshape(b * t, d)
    q, _, _, rows_t, win_t, kk, vt, gates, ret = _proj(xp2, ln1r, w_perm, *_rope_tables(jnp.arange(t)), tm)
    kcb, vcbt = _compress_prompt(rows_t, cpos, cw1, cw2)
    nsel_p = t // SEL_BLK
    nb16_p = t // CMP_STRIDE
    eoh = (jnp.arange(t)[:, None] // SEL_BLK == jnp.arange(LANES)[None, :]).astype(_MXU)
    o_nsa = _nsa_prompt(q.reshape(b, t, NSA_W), gates.reshape(b, t, LANES), kcb, vcbt,
                        kk.reshape(b, t, 2 * KV_W), vt, eoh, _cover_matrix(nsel_p, nb16_p, nsel_p),
                        2 if b % 2 == 0 else 1).reshape(b * t, NSA_W)
    o_ret, st_p = _ret_prompt(ret.reshape(b, t, 4 * RET_W), ret_tabs, 4 if b % 4 == 0 else 1)
    o_ret = o_ret.reshape(b * t, RET_W)
    y_prompt = _post(xp2, o_nsa, o_ret, wo, ln2r, wg, wu, wd, lnfr, tm).reshape(b, t, d)
    to_cache = lambda a: jnp.transpose(a.reshape(a.shape[0], a.shape[1], KV_GROUPS, HEAD_DIM, a.shape[3]),
                                       (0, 4, 1, 2, 3))[None]
    kv_prompt = to_cache(rows_t)
    win_keep = min(WINDOW, t)
    win_prompt = to_cache(win_t[:, :, :, t - win_keep:])
    ret_prompt = st_p[None]

    xs2 = x_sample.reshape(b2, d)
    tms = min(tm, b2)
    pos_s = jnp.full((b2,), past, jnp.int32)
    q_s, rows_s, win_s, rows_st, _, _, _, gates_s, ret_s = _proj(xs2, ln1r, w_perm, *_rope_tables(pos_s), tms)
    n_pool = cache_kv.shape[1]
    pool_t = jnp.transpose(cache_kv[l], (0, 2, 3, 4, 1)).reshape(n_pool, 4, KV_W, PAGE_SIZE)
    cache_win_t = jnp.transpose(cache_win[l], (0, 2, 3, 4, 1)).reshape(b2, 2, KV_W, wbuf)
    pt_flat = page_table.reshape(-1).astype(jnp.int32)
    kcb_s, vcb_s = _compress_sample(pool_t, pt_flat, b2, n_pages, cpos, cw1, cw2)
    seq_len = past + 1
    nsel_s = -(-seq_len // SEL_BLK)
    nsel_pad = -(-nsel_s // LANES) * LANES
    nb16_s = past // CMP_STRIDE
    q3 = q_s.astype(_F32).reshape(b2, 1, NSA_W)
    idx8, oc = _s_topk(q3, kcb_s, vcb_s, _cover_matrix(nsel_s, nb16_s, nsel_pad), past, nsel_s)
    n_top = min(SEL_TOPN, nsel_s)
    idx_flat = idx8[:, :n_top].reshape(-1)
    o_nsa_s = _s_attn(idx_flat, pt_flat, pool_t, q3, gates_s.reshape(b2, 1, LANES), oc,
                      rows_s.reshape(b2, 1, 4 * KV_W), win_s.reshape(b2, 1, 2 * KV_W), cache_win_t,
                      past, nsel_s, past, n_pages).reshape(b2, NSA_W).astype(_MXU)
    nr = b2 * RET_HEADS
    heads = lambda a: a.reshape(nr, HEAD_DIM)
    rq, rk, rv, rg = (heads(ret_s[:, k * RET_W:(k + 1) * RET_W]) for k in range(4))
    gam = jnp.tile(jnp.exp(lg), (b2,))[:, None]
    dd = HEAD_DIM * HEAD_DIM
    lane_d = jnp.arange(dd) // HEAD_DIM
    lane_e = jnp.arange(dd) % HEAD_DIM
    e1 = (jnp.arange(HEAD_DIM)[:, None] == lane_d[None, :]).astype(_MXU)
    e2 = (jnp.arange(HEAD_DIM)[:, None] == lane_e[None, :]).astype(_MXU)
    o_ret_s, st_s = _s_ret(rq, rk, rv, rg, gam, state_ret[l].reshape(nr, dd), e1, e2, min(128, nr))
    y_sample = _post(xs2, o_nsa_s, o_ret_s.reshape(b2, RET_W), wo, ln2r, wg, wu, wd, lnfr, tms).reshape(b2, 1, d)
    kv_sample = jnp.transpose(rows_st.reshape(4, KV_GROUPS, HEAD_DIM, b2), (3, 0, 1, 2))[None, :, None]
    win_sample = jnp.concatenate([cache_win[l][:, 1:], win_s.reshape(b2, 1, 2, KV_GROUPS, HEAD_DIM)], axis=1)[None]
    ret_sample = st_s.reshape(1, b2, RET_HEADS, HEAD_DIM, HEAD_DIM)
    return (y_prompt, y_sample, kv_prompt, kv_sample, win_prompt, win_sample, ret_prompt, ret_sample)
```

```python
import functools

import numpy as np
import jax
import jax.numpy as jnp
from jax import lax
from jax.experimental import pallas as pl
from jax.experimental.pallas import tpu as pltpu

HEAD_DIM = 64
NSA_HEADS = 8
RET_HEADS = 8
KV_GROUPS = 2
HPG = NSA_HEADS // KV_GROUPS
CMP_BLK = 32
CMP_STRIDE = 16
CMP_HID = 4 * HEAD_DIM
SEL_BLK = 64
SEL_TOPN = 16
WINDOW = 512
Q_BLK = 128
RET_CHUNK = 256
PAGE_SIZE = 128
ROPE_THETA = 10000.0
RMS_EPS = 1e-6

LANES = 128
NSA_W = NSA_HEADS * HEAD_DIM
RET_W = RET_HEADS * HEAD_DIM
KV_W = KV_GROUPS * HEAD_DIM
NEG_BIG = -(2.0 ** 100)
LOG2E = 1.4426950408889634
VT_ROWS = KV_W + 16
VMEM_LIMIT = 56 * 1024 * 1024

_MXU = jnp.bfloat16
_F32 = jnp.float32


def _dot(a, b):
    return jnp.dot(a.astype(_MXU), b.astype(_MXU), preferred_element_type=_F32)


def _dot_nt(a, b):
    return lax.dot_general(a.astype(_MXU), b.astype(_MXU), (((1,), (1,)), ((), ())),
                           preferred_element_type=_F32)


def _dot_tn(a, b):
    return lax.dot_general(a.astype(_MXU), b.astype(_MXU), (((0,), (0,)), ((), ())),
                           preferred_element_type=_F32)


def _split3(x):
    hi = x.astype(_MXU)
    r1 = x - hi.astype(_F32)
    mid = r1.astype(_MXU)
    lo = (r1 - mid.astype(_F32)).astype(_MXU)
    return hi, mid, lo


def _params(sem):
    return pltpu.CompilerParams(dimension_semantics=sem, vmem_limit_bytes=VMEM_LIMIT)


_C_Q, _C_KV, _C_RQ, _C_RK, _C_RV, _C_RG, _C_GT = 0, 512, 1280, 1792, 2304, 2816, 3328
_PROJ_COLS = 3456


def _proj_kernel(x_ref, g_ref, w_ref, cos_ref, sa_ref, sb_ref,
                 q_ref, rows_ref, win_ref, rows_t_ref, win_t_ref, kk_ref, vt_ref, gates_ref, ret_ref):
    x = x_ref[...]
    ms = jnp.mean(x * x, axis=-1, keepdims=True)
    h = (x * lax.rsqrt(ms + RMS_EPS) * g_ref[...]).astype(_MXU)
    cos, sa, sb = cos_ref[...], sa_ref[...], sb_ref[...]

    def seg(c0, n):
        return jnp.dot(h, w_ref[:, c0:c0 + n], preferred_element_type=_F32)

    def rope(p):
        return p * cos + pltpu.roll(p, LANES - 32, 1) * sa + pltpu.roll(p, 32, 1) * sb

    def slab(p, s):
        return p[:, s * LANES:(s + 1) * LANES]

    scale = HEAD_DIM ** -0.5
    p = seg(_C_Q, NSA_W)
    for s in range(4):
        q_ref[:, s * LANES:(s + 1) * LANES] = (rope(slab(p, s)) * (scale * LOG2E)).astype(q_ref.dtype)
    p = seg(_C_KV, 6 * KV_W)
    kc, vc = rope(slab(p, 0)), slab(p, 1)
    ks, vs = rope(slab(p, 2)), slab(p, 3)
    kw, vw = rope(slab(p, 4)), slab(p, 5)
    rows_ref[:, 0:128] = kc
    rows_ref[:, 128:256] = vc
    rows_ref[:, 256:384] = ks
    rows_ref[:, 384:512] = vs
    win_ref[:, 0:128] = kw
    win_ref[:, 128:256] = vw
    kk_ref[:, 0:128] = ks.astype(kk_ref.dtype)
    kk_ref[:, 128:256] = kw.astype(kk_ref.dtype)
    vs_t, vw_t = vs.T, vw.T
    rows_t_ref[0] = kc.T
    rows_t_ref[1] = vc.T
    rows_t_ref[2] = ks.T
    rows_t_ref[3] = vs_t
    win_t_ref[0] = kw.T
    win_t_ref[1] = vw_t
    ones = jnp.ones((VT_ROWS - KV_W, vs_t.shape[1]), vt_ref.dtype)
    vt_ref[0, 0:KV_W] = vs_t.astype(vt_ref.dtype)
    vt_ref[0, KV_W:VT_ROWS] = ones
    vt_ref[1, 0:KV_W] = vw_t.astype(vt_ref.dtype)
    vt_ref[1, KV_W:VT_ROWS] = ones
    p = seg(_C_RQ, RET_W)
    for s in range(4):
        ret_ref[:, s * LANES:(s + 1) * LANES] = rope(slab(p, s))
    p = seg(_C_RK, RET_W)
    for s in range(4):
        ret_ref[:, RET_W + s * LANES:RET_W + (s + 1) * LANES] = rope(slab(p, s)) * scale
    ret_ref[:, 2 * RET_W:3 * RET_W] = seg(_C_RV, RET_W)
    ret_ref[:, 3 * RET_W:4 * RET_W] = seg(_C_RG, RET_W)
    gates_ref[...] = jax.nn.sigmoid(seg(_C_GT, LANES))


def _proj(x2, ln, w_perm, cos, sa, sb, tm):
    n, d = x2.shape
    tt = cos.shape[0]
    nt = tt // tm
    row = lambda i: (i, 0)
    tab = lambda i: (i % nt, 0)
    const = lambda i: (0, 0)
    out_shape = (
        jax.ShapeDtypeStruct((n, NSA_W), _MXU),
        jax.ShapeDtypeStruct((n, 4 * KV_W), _F32),
        jax.ShapeDtypeStruct((n, 2 * KV_W), _F32),
        jax.ShapeDtypeStruct((n // tt, 4, KV_W, tt), _F32),
        jax.ShapeDtypeStruct((n // tt, 2, KV_W, tt), _F32),
        jax.ShapeDtypeStruct((n, 2 * KV_W), _MXU),
        jax.ShapeDtypeStruct((n // tt, 2, VT_ROWS, tt), _MXU),
        jax.ShapeDtypeStruct((n, LANES), _F32),
        jax.ShapeDtypeStruct((n, 4 * RET_W), _F32),
    )
    return pl.pallas_call(
        _proj_kernel,
        grid=(n // tm,),
        in_specs=[
            pl.BlockSpec((tm, d), row),
            pl.BlockSpec((1, d), const),
            pl.BlockSpec((d, _PROJ_COLS), const),
            pl.BlockSpec((tm, LANES), tab),
            pl.BlockSpec((tm, LANES), tab),
            pl.BlockSpec((tm, LANES), tab),
        ],
        out_specs=(
            pl.BlockSpec((tm, NSA_W), row),
            pl.BlockSpec((tm, 4 * KV_W), row),
            pl.BlockSpec((tm, 2 * KV_W), row),
            pl.BlockSpec((None, 4, KV_W, tm), lambda i: (i // nt, 0, 0, i % nt)),
            pl.BlockSpec((None, 2, KV_W, tm), lambda i: (i // nt, 0, 0, i % nt)),
            pl.BlockSpec((tm, 2 * KV_W), row),
            pl.BlockSpec((None, 2, VT_ROWS, tm), lambda i: (i // nt, 0, 0, i % nt)),
            pl.BlockSpec((tm, LANES), row),
            pl.BlockSpec((tm, 4 * RET_W), row),
        ),
        out_shape=out_shape,
        compiler_params=_params(("parallel",)),
        name="proj",
    )(x2, ln, w_perm, cos, sa, sb)


def _post_kernel(x_ref, on_ref, or_ref, wo_ref, g2_ref, wg_ref, wu_ref, wd_ref, gf_ref, y_ref):
    x = x_ref[...]
    mix = (jnp.dot(on_ref[...], wo_ref[0:NSA_W, :], preferred_element_type=_F32)
           + jnp.dot(or_ref[...], wo_ref[NSA_W:NSA_W + RET_W, :], preferred_element_type=_F32))
    x1 = x + mix
    ms = jnp.mean(x1 * x1, axis=-1, keepdims=True)
    h = (x1 * lax.rsqrt(ms + RMS_EPS) * g2_ref[...]).astype(_MXU)
    a = jax.nn.silu(jnp.dot(h, wg_ref[...], preferred_element_type=_F32))
    a = a * jnp.dot(h, wu_ref[...], preferred_element_type=_F32)
    y = x1 + jnp.dot(a.astype(_MXU), wd_ref[...], preferred_element_type=_F32)
    ms = jnp.mean(y * y, axis=-1, keepdims=True)
    y_ref[...] = y * lax.rsqrt(ms + RMS_EPS) * gf_ref[...]


def _post(x2, o_nsa, o_ret, w_out, ln2, w_gate, w_up, w_down, ln_f, tm):
    n, d = x2.shape
    dff = w_gate.shape[1]
    row = lambda i: (i, 0)
    const = lambda i: (0, 0)
    once = dict(pipeline_mode=pl.Buffered(1))
    return pl.pallas_call(
        _post_kernel,
        grid=(n // tm,),
        in_specs=[
            pl.BlockSpec((tm, d), row),
            pl.BlockSpec((tm, NSA_W), row),
            pl.BlockSpec((tm, RET_W), row),
            pl.BlockSpec((d, d), const, **once),
            pl.BlockSpec((1, d), const),
            pl.BlockSpec((d, dff), const, **once),
            pl.BlockSpec((d, dff), const, **once),
            pl.BlockSpec((dff, d), const, **once),
            pl.BlockSpec((1, d), const),
        ],
        out_specs=pl.BlockSpec((tm, d), row),
        out_shape=jax.ShapeDtypeStruct((n, d), _F32),
        compiler_params=_params(("parallel",)),
        name="post",
    )(x2, o_nsa, o_ret, w_out, ln2, w_gate, w_up, w_down, ln_f)


def _fill_token_rows(planes, buf_ref, c):
    for j, plane in enumerate(planes):
        buf_ref[c, j * LANES:(j + 1) * LANES, :] = plane.T


def _compress_rows(buf_ref, c, nb16, pos_ref, w1_ref, w2_ref):
    half = CMP_STRIDE // 2
    a = jnp.zeros((nb16, 2 * CMP_HID), _F32)
    b = jnp.zeros((nb16, 2 * CMP_HID), _F32)
    for tp in range(half):
        t0, t1 = 2 * tp, 2 * tp + 1
        x0 = buf_ref[c, pl.ds(t0, nb16, stride=CMP_STRIDE), :]
        x1 = buf_ref[c, pl.ds(t1, nb16, stride=CMP_STRIDE), :]
        xa = jnp.concatenate([x0 + pos_ref[c, t0:t0 + 1, :], x1 + pos_ref[c, t1:t1 + 1, :]], axis=1)
        xb = jnp.concatenate([x0 + pos_ref[c, CMP_STRIDE + t0:CMP_STRIDE + t0 + 1, :],
                              x1 + pos_ref[c, CMP_STRIDE + t1:CMP_STRIDE + t1 + 1, :]], axis=1)
        a = a + _dot(xa, w1_ref[c, tp])
        b = b + _dot(xb, w1_ref[c, half + tp])
    hid = a + pltpu.roll(b, nb16 - 1, 0)
    return _dot(jax.nn.gelu(hid), w2_ref[c])


def _compress_prompt_kernel(rt_ref, pos_ref, w1_ref, w2_ref, kcb_ref, vcbt_ref, buf_ref):
    nb16 = kcb_ref.shape[0]
    outs = []
    for c in range(2):
        planes = [rt_ref[c, :, j * LANES:(j + 1) * LANES] for j in range(nb16 * CMP_STRIDE // LANES)]
        _fill_token_rows(planes, buf_ref, c)
        outs.append(_compress_rows(buf_ref, c, nb16, pos_ref, w1_ref, w2_ref))
    kcb_ref[...] = outs[0]
    vcbt_ref[...] = outs[1].T


def _compress_prompt(rows_t, pos, w1bd, w2bd):
    b, _, _, t = rows_t.shape
    nb16 = t // CMP_STRIDE
    c3 = lambda i: (0, 0, 0)
    c4 = lambda i: (0, 0, 0, 0)
    out = jax.ShapeDtypeStruct((b, nb16, LANES), _F32)
    return pl.pallas_call(
        _compress_prompt_kernel,
        grid=(b,),
        in_specs=[
            pl.BlockSpec((None, 2, KV_W, t), lambda i: (i, 0, 0, 0)),
            pl.BlockSpec(pos.shape, c3),
            pl.BlockSpec(w1bd.shape, c4),
            pl.BlockSpec(w2bd.shape, c3),
        ],
        out_specs=(pl.BlockSpec((None, nb16, LANES), lambda i: (i, 0, 0)),
                   pl.BlockSpec((None, LANES, nb16), lambda i: (i, 0, 0))),
        out_shape=(out, jax.ShapeDtypeStruct((b, LANES, nb16), _F32)),
        scratch_shapes=[pltpu.VMEM((2, t, LANES), _F32)],
        compiler_params=_params(("parallel",)),
        name="compress_prompt",
    )(rows_t, pos, w1bd, w2bd)


def _compress_sample_kernel(n_pages, pt_ref, *refs):
    page_refs = refs[:n_pages]
    pos_ref, w1_ref, w2_ref, kcb_ref, vcb_ref, buf_ref = refs[n_pages:]
    nb16 = kcb_ref.shape[0]
    for c, out_ref in enumerate((kcb_ref, vcb_ref)):
        _fill_token_rows([page_refs[j][c] for j in range(n_pages)], buf_ref, c)
        out_ref[...] = _compress_rows(buf_ref, c, nb16, pos_ref, w1_ref, w2_ref)


def _compress_sample(pool_t, page_table_flat, b2, n_pages, pos, w1bd, w2bd):
    past = n_pages * PAGE_SIZE
    nb16 = past // CMP_STRIDE
    c3 = lambda i, pt: (0, 0, 0)
    c4 = lambda i, pt: (0, 0, 0, 0)

    def page_spec(j):
        return pl.BlockSpec((None, 2, KV_W, PAGE_SIZE), lambda i, pt: (pt[i * n_pages + j], 0, 0, 0))

    out = jax.ShapeDtypeStruct((b2, nb16, LANES), _F32)
    grid_spec = pltpu.PrefetchScalarGridSpec(
        num_scalar_prefetch=1,
        grid=(b2,),
        in_specs=[page_spec(j) for j in range(n_pages)] + [
            pl.BlockSpec(pos.shape, c3),
            pl.BlockSpec(w1bd.shape, c4),
            pl.BlockSpec(w2bd.shape, c3),
        ],
        out_specs=(pl.BlockSpec((None, nb16, LANES), lambda i, pt: (i, 0, 0)),
                   pl.BlockSpec((None, nb16, LANES), lambda i, pt: (i, 0, 0))),
        scratch_shapes=[pltpu.VMEM((2, past, LANES), _F32)],
    )
    return pl.pallas_call(
        functools.partial(_compress_sample_kernel, n_pages),
        grid_spec=grid_spec,
        out_shape=(out, out),
        compiler_params=_params(("arbitrary",)),
        name="compress_sample",
    )(page_table_flat, *([pool_t] * n_pages), pos, w1bd, w2bd)


def _group_queries(qf, rows):
    lane = lax.broadcasted_iota(jnp.int32, (rows, LANES), 1)
    out = []
    for h in range(NSA_HEADS):
        g = h // HPG
        s = qf[:, (h // 2) * LANES:(h // 2 + 1) * LANES]
        if (h % 2) != g:
            s = pltpu.roll(s, HEAD_DIM, 1)
        out.append(jnp.where((lane >= g * HEAD_DIM) & (lane < (g + 1) * HEAD_DIM), s, 0.0))
    return out


def _masked_softmax(s, mask):
    s = jnp.where(mask, s, -jnp.inf)
    m = jnp.max(s, axis=-1, keepdims=True)
    m = jnp.where(m > -jnp.inf, m, 0.0)
    e = jnp.where(mask, jnp.exp2(s - m), 0.0)
    return e / jnp.maximum(jnp.sum(e, axis=-1, keepdims=True), 1e-30)


def _place_heads(o_heads, rows):
    lane = lax.broadcasted_iota(jnp.int32, (rows, LANES), 1)
    slabs = []
    for k in range(NSA_HEADS // 2):
        pair = []
        for h in (2 * k, 2 * k + 1):
            g = h // HPG
            o = o_heads[h]
            if (h % 2) != g:
                o = pltpu.roll(o, HEAD_DIM, 1)
            pair.append(o)
        slabs.append(jnp.where(lane < HEAD_DIM, pair[0], pair[1]))
    return slabs


def _softmax_keys(s):
    m = jnp.max(s, axis=0, keepdims=True)
    m = jnp.where(m > -jnp.inf, m, 0.0)
    e = jnp.exp2(s - m)
    return e * (1.0 / jnp.maximum(jnp.sum(e, axis=0, keepdims=True), 1e-30))


def _weights_keys(s):
    m = jnp.max(s, axis=0, keepdims=True)
    m = jnp.where(m > -jnp.inf, m, 0.0)
    return jnp.exp2((s - m).astype(_MXU))


def _normalise(o_aug):
    return o_aug[0:KV_W] * (1.0 / jnp.maximum(o_aug[KV_W:KV_W + 1], 1e-30))


def _nsa_prompt_kernel(nsel, q_ref, gates_ref, kcb_ref, vcbt_ref, kk_ref, vt_ref, eoh_ref, cover_ref,
                       o_ref, acc_ref, m_ref, s_ref, qa_ref):
    i = pl.program_id(1)
    nseq = q_ref.shape[0]
    seqs = range(nseq)
    tq = Q_BLK
    qpos_1 = i * tq + lax.broadcasted_iota(jnp.int32, (1, tq), 1)

    def per_head(x):
        return jnp.concatenate([x] * NSA_HEADS, axis=1)

    nb16 = kcb_ref.shape[1]
    blk_end = lax.broadcasted_iota(jnp.int32, (nb16, 1), 0) * CMP_STRIDE + (CMP_BLK - 1)
    bias_c = per_head(jnp.where(blk_end <= qpos_1, 0.0, -jnp.inf))
    q2 = [jnp.concatenate(_group_queries(q_ref[sq].astype(_F32), tq), axis=0).astype(_MXU)
          for sq in seqs]
    p_c = [_softmax_keys(_dot_nt(kcb_ref[sq], q2[sq]) + bias_c) for sq in seqs]
    o_c = [_dot(vcbt_ref[sq], p_c[sq]) for sq in seqs]

    jidx = lax.broadcasted_iota(jnp.int32, (nsel, tq), 0)
    qpos_l = i * tq + lax.broadcasted_iota(jnp.int32, (nsel, tq), 1)
    cur = qpos_l // SEL_BLK
    forced = (jidx == 0) | (jidx == cur) | (jidx == cur - 1)
    valid = jidx * SEL_BLK <= qpos_l
    n_top = min(SEL_TOPN, nsel)
    sub = lax.broadcasted_iota(jnp.int32, (8, tq), 0)
    nv = nsel // 8

    def choice_bias(p_cs, g):
        psum = p_cs[:, (g * HPG) * tq:(g * HPG + 1) * tq]
        for hh in range(1, HPG):
            psum = psum + p_cs[:, (g * HPG + hh) * tq:(g * HPG + hh + 1) * tq]
        imp_t = jnp.zeros((nsel, tq), _F32)
        for part in _split3(psum):
            imp_t = imp_t + jnp.dot(cover_ref[...], part, preferred_element_type=_F32)
        score = jnp.where(forced, jnp.inf, jnp.where(valid, imp_t, -jnp.inf))
        sc_v = [score[8 * v:8 * v + 8] for v in range(nv)]
        rank_v = [jnp.zeros((8, tq), jnp.int32) for _ in range(nv)]
        for jp in range(nsel):
            rowb = jnp.broadcast_to(score[jp:jp + 1, :], (8, tq))
            for v in range(nv):
                if v > jp // 8:
                    beats = jnp.where(rowb >= sc_v[v], 1, 0)
                elif v < jp // 8:
                    beats = jnp.where(rowb > sc_v[v], 1, 0)
                else:
                    beats = jnp.where(sub > (jp % 8), jnp.where(rowb >= sc_v[v], 1, 0),
                                      jnp.where(rowb > sc_v[v], 1, 0))
                rank_v[v] = rank_v[v] + beats
        sel_t = jnp.concatenate([jnp.where(r < n_top, 1.0, 0.0) for r in rank_v], axis=0)
        if nsel < LANES:
            sel_t = jnp.concatenate([sel_t, jnp.ones((LANES - nsel, tq), _F32)], axis=0)
        return ((sel_t.T - 1.0) * (-NEG_BIG)).astype(_MXU)

    for sq in seqs:
        qa_ref[sq, :, 0:LANES] = q2[sq]
        for g in range(KV_GROUPS):
            bias = choice_bias(p_c[sq], g)
            for hh in range(HPG):
                h = g * HPG + hh
                qa_ref[sq, h * tq:(h + 1) * tq, LANES:2 * LANES] = bias

    tk = 2 * tq
    acc_ref[...] = jnp.zeros(acc_ref.shape, _F32)
    m_ref[...] = jnp.full(m_ref.shape, 2.0 * NEG_BIG, _F32)

    def scores(slot, kt):
        k0 = pl.multiple_of(kt * tk, tk)
        eoh = eoh_ref[pl.ds(k0, tk), :]
        for sq in seqs:
            k_aug = jnp.concatenate([kk_ref[sq, pl.ds(k0, tk), 0:LANES], eoh], axis=1)
            s_ref[sq, slot] = _dot_nt(k_aug, qa_ref[sq])

    def consume(slot, kt, causal):
        k0 = pl.multiple_of(kt * tk, tk)
        if causal:
            kpos = k0 + lax.broadcasted_iota(jnp.int32, (tk, 1), 0)
            cbias = per_head(jnp.where(kpos <= qpos_1, 0.0, NEG_BIG))
        for sq in seqs:
            s = s_ref[sq, slot]
            if causal:
                s = s + cbias
            m_old = m_ref[sq, slot]
            m_new = jnp.maximum(m_old, jnp.max(s, axis=0, keepdims=True))
            alpha = jnp.exp2(m_old - m_new)
            p = jnp.exp2((s - m_new).astype(_MXU))
            m_ref[sq, slot] = m_new
            acc_ref[sq, slot] = alpha * acc_ref[sq, slot] + _dot(vt_ref[sq, 0, :, pl.ds(k0, tk)], p)

    last = (i + 2) // 2 - 1
    scores(0, 0)

    def tile_pair(j, carry):
        scores(1, 2 * j + 1)
        consume(0, 2 * j, False)
        scores(0, 2 * j + 2)
        consume(1, 2 * j + 1, False)
        return carry

    lax.fori_loop(0, last // 2, tile_pair, 0)

    @pl.when(last % 2 == 1)
    def _():
        scores(1, last)
        consume(0, last - 1, False)
        consume(1, last, True)

    @pl.when(last % 2 == 0)
    def _():
        consume(0, last, True)

    wk = WINDOW + tq
    start = pl.multiple_of(jnp.maximum(i * tq - WINDOW, 0), tq)
    wpos = start + lax.broadcasted_iota(jnp.int32, (wk, 1), 0)
    bias_w = per_head(jnp.where(wpos <= qpos_1, jnp.where(wpos > qpos_1 - WINDOW, 0.0, -jnp.inf), -jnp.inf))
    for sq in seqs:
        m_all = jnp.maximum(m_ref[sq, 0], m_ref[sq, 1])
        o_s = _normalise(jnp.exp2(m_ref[sq, 0] - m_all) * acc_ref[sq, 0]
                         + jnp.exp2(m_ref[sq, 1] - m_all) * acc_ref[sq, 1])
        e_w = _weights_keys(_dot_nt(kk_ref[sq, pl.ds(start, wk), LANES:2 * LANES], q2[sq]) + bias_w)
        o_w = _normalise(_dot(vt_ref[sq, 1, :, pl.ds(start, wk)], e_w))

        gt_t = gates_ref[sq].T
        o_rows = []
        for h in range(NSA_HEADS):
            g = h // HPG
            rs = slice(g * HEAD_DIM, (g + 1) * HEAD_DIM)
            cs = slice(h * tq, (h + 1) * tq)
            o_rows.append(gt_t[3 * h:3 * h + 1] * o_c[sq][rs, cs] + gt_t[3 * h + 1:3 * h + 2] * o_s[rs, cs]
                          + gt_t[3 * h + 2:3 * h + 3] * o_w[rs, cs])
        o_ref[sq] = jnp.concatenate(o_rows, axis=0).T.astype(o_ref.dtype)


def _nsa_prompt(q3, gates3, kcb, vcbt, kk3, vt4, eoh, cover, nseq):
    b, t, _ = q3.shape
    nq = t // Q_BLK
    nb16 = t // CMP_STRIDE
    nsel = t // SEL_BLK
    qblk = lambda bi, i: (bi, i, 0)
    per_b = lambda bi, i: (bi, 0, 0)
    const = lambda bi, i: (0, 0)
    ncol = NSA_HEADS * Q_BLK
    return pl.pallas_call(
        functools.partial(_nsa_prompt_kernel, nsel),
        grid=(b // nseq, nq),
        in_specs=[
            pl.BlockSpec((nseq, Q_BLK, NSA_W), qblk),
            pl.BlockSpec((nseq, Q_BLK, LANES), qblk),
            pl.BlockSpec((nseq, nb16, LANES), per_b),
            pl.BlockSpec((nseq, LANES, nb16), per_b),
            pl.BlockSpec((nseq, t, 2 * KV_W), per_b),
            pl.BlockSpec((nseq, 2, VT_ROWS, t), lambda bi, i: (bi, 0, 0, 0)),
            pl.BlockSpec((t, LANES), const),
            pl.BlockSpec((nsel, nb16), const),
        ],
        out_specs=pl.BlockSpec((nseq, Q_BLK, NSA_W), qblk),
        out_shape=jax.ShapeDtypeStruct((b, t, NSA_W), _MXU),
        scratch_shapes=[pltpu.VMEM((nseq, 2, VT_ROWS, ncol), _F32),
                        pltpu.VMEM((nseq, 2, 1, ncol), _F32),
                        pltpu.VMEM((nseq, 2, 2 * Q_BLK, ncol), _F32),
                        pltpu.VMEM((nseq, ncol, 2 * LANES), _MXU)],
        compiler_params=_params(("parallel", "arbitrary")),
        name="nsa_prompt",
    )(q3, gates3, kcb, vcbt, kk3, vt4, eoh, cover)


def _ret_prompt_kernel(ret_ref, dmat_ref, rowdec_ref, kdec_ref, sdec_ref, bmask_ref,
                       o_ref, st_ref, s_ref):
    c = pl.program_id(1)
    nc = pl.num_programs(1)
    nseq = ret_ref.shape[0]
    tq = RET_CHUNK

    @pl.when(c == 0)
    def _():
        s_ref[...] = jnp.zeros(s_ref.shape, _F32)

    lane = lax.broadcasted_iota(jnp.int32, (tq, LANES), 1)
    lo = lane < HEAD_DIM
    bmask = bmask_ref[...]
    for pr in range(RET_HEADS // 2):
        c0 = pr * LANES
        dm = jnp.concatenate([dmat_ref[2 * pr], dmat_ref[2 * pr + 1]], axis=0)
        for sq in range(nseq):
            q = ret_ref[sq, :, c0:c0 + LANES]
            k = ret_ref[sq, :, RET_W + c0:RET_W + c0 + LANES]
            v = ret_ref[sq, :, 2 * RET_W + c0:2 * RET_W + c0 + LANES]
            g = ret_ref[sq, :, 3 * RET_W + c0:3 * RET_W + c0 + LANES]
            q2 = jnp.concatenate([jnp.where(lo, q, 0.0), jnp.where(lo, 0.0, q)], axis=0)
            o2 = _dot(_dot_nt(q2, k) * dm, v)
            o = jnp.where(lo, o2[0:tq], o2[tq:2 * tq])
            s_old = s_ref[sq, pr]
            o = o + _dot(q, s_old) * rowdec_ref[pr]
            s_ref[sq, pr] = s_old * sdec_ref[pr] + _dot_tn(k * kdec_ref[pr], v) * bmask
            o_sq = o * o
            s0 = jnp.sum(jnp.where(lo, o_sq, 0.0), axis=-1, keepdims=True)
            s1 = jnp.sum(jnp.where(lo, 0.0, o_sq), axis=-1, keepdims=True)
            ms = jnp.where(lo, s0, s1) * (1.0 / HEAD_DIM)
            o = o * lax.rsqrt(ms + RMS_EPS) * jax.nn.silu(g)
            o_ref[sq, :, c0:c0 + LANES] = o.astype(o_ref.dtype)

    @pl.when(c == nc - 1)
    def _():
        for sq in range(nseq):
            for pr in range(RET_HEADS // 2):
                s_fin = s_ref[sq, pr]
                st_ref[sq, 2 * pr] = s_fin[0:HEAD_DIM, 0:HEAD_DIM]
                st_ref[sq, 2 * pr + 1] = pltpu.roll(s_fin, HEAD_DIM, 1)[HEAD_DIM:2 * HEAD_DIM, 0:HEAD_DIM]


def _ret_prompt(ret3, tabs, nseq):
    b, t, _ = ret3.shape
    nc = t // RET_CHUNK
    blk = lambda bi, i: (bi, i, 0)
    c3 = lambda bi, i: (0, 0, 0)
    dmat, rowdec, kdec, sdec, bmask = tabs
    return pl.pallas_call(
        _ret_prompt_kernel,
        grid=(b // nseq, nc),
        in_specs=[
            pl.BlockSpec((nseq, RET_CHUNK, 4 * RET_W), blk),
            pl.BlockSpec(dmat.shape, c3),
            pl.BlockSpec(rowdec.shape, c3),
            pl.BlockSpec(kdec.shape, c3),
            pl.BlockSpec(sdec.shape, c3),
            pl.BlockSpec(bmask.shape, lambda bi, i: (0, 0)),
        ],
        out_specs=(pl.BlockSpec((nseq, RET_CHUNK, RET_W), blk),
                   pl.BlockSpec((nseq, RET_HEADS, HEAD_DIM, HEAD_DIM), lambda bi, i: (bi, 0, 0, 0))),
        out_shape=(jax.ShapeDtypeStruct((b, t, RET_W), _MXU),
                   jax.ShapeDtypeStruct((b, RET_HEADS, HEAD_DIM, HEAD_DIM), _F32)),
        scratch_shapes=[pltpu.VMEM((nseq, RET_HEADS // 2, LANES, LANES), _F32)],
        compiler_params=_params(("parallel", "arbitrary")),
        name="ret_prompt",
    )(ret3, dmat, rowdec, kdec, sdec, bmask)


def _sample_queries(q_row):
    heads = _group_queries(q_row.astype(_F32), 1)
    row = lax.broadcasted_iota(jnp.int32, (NSA_HEADS, LANES), 0)
    q8 = jnp.zeros((NSA_HEADS, LANES), _F32)
    for h in range(NSA_HEADS):
        q8 = jnp.where(row == h, jnp.broadcast_to(heads[h], (NSA_HEADS, LANES)), q8)
    return q8


def _s_topk_kernel(q_pos, nsel, q_ref, kcb_ref, vcb_ref, cover_ref, idx_ref, oc_ref):
    nseq, nb16, _ = kcb_ref.shape
    nrow = nseq * KV_GROUPS
    blk_end = lax.broadcasted_iota(jnp.int32, (NSA_HEADS, nb16), 1) * CMP_STRIDE + (CMP_BLK - 1)
    row = lax.broadcasted_iota(jnp.int32, (nrow, nb16), 0)
    psum = jnp.zeros((nrow, nb16), _F32)
    for s in range(nseq):
        q8 = _sample_queries(q_ref[s])
        p_c = _masked_softmax(_dot_nt(q8, kcb_ref[s]), blk_end <= q_pos)
        oc_ref[s] = _dot(p_c, vcb_ref[s])
        for g in range(KV_GROUPS):
            acc = p_c[g * HPG:g * HPG + 1]
            for hh in range(1, HPG):
                acc = acc + p_c[g * HPG + hh:g * HPG + hh + 1]
            psum = jnp.where(row == KV_GROUPS * s + g, jnp.broadcast_to(acc, (nrow, nb16)), psum)

    npad = cover_ref.shape[0]
    imp = jnp.zeros((nrow, npad), _F32)
    for part in _split3(psum):
        imp = imp + lax.dot_general(part, cover_ref[...], (((1,), (1,)), ((), ())),
                                    preferred_element_type=_F32)
    j = lax.broadcasted_iota(jnp.int32, (nrow, npad), 1)
    cur = q_pos // SEL_BLK
    forced = (j == 0) | (j == cur) | (j == cur - 1)
    valid = j * SEL_BLK <= q_pos
    score = jnp.where(forced, jnp.inf, jnp.where(valid, imp, -jnp.inf))
    jf = j.astype(_F32)
    alive = jnp.where(j < nsel, 1.0, 0.0)
    lane = lax.broadcasted_iota(jnp.int32, (nrow, LANES), 1)
    idx = jnp.zeros((nrow, LANES), _F32)
    for r in range(min(SEL_TOPN, nsel)):
        live = alive > 0.0
        m = jnp.max(jnp.where(live, score, -jnp.inf), axis=-1, keepdims=True)
        cand = jnp.where(live, jnp.where(score == m, jf, float(npad)), float(npad))
        jmin = jnp.min(cand, axis=-1, keepdims=True)
        idx = jnp.where(lane == r, jmin, idx)
        alive = jnp.where(jf == jmin, 0.0, alive)
    idx_ref[...] = idx.astype(jnp.int32)


def _s_topk(q3, kcb, vcb, cover_s, q_pos, nsel):
    b2 = q3.shape[0]
    nb16 = kcb.shape[1]
    nseq = 8 if b2 % 8 == 0 else b2
    per_b = lambda i: (i, 0, 0)
    return pl.pallas_call(
        functools.partial(_s_topk_kernel, q_pos, nsel),
        grid=(b2 // nseq,),
        in_specs=[
            pl.BlockSpec((nseq, 1, NSA_W), per_b),
            pl.BlockSpec((nseq, nb16, LANES), per_b),
            pl.BlockSpec((nseq, nb16, LANES), per_b),
            pl.BlockSpec(cover_s.shape, lambda i: (0, 0)),
        ],
        out_specs=(pl.BlockSpec((nseq * KV_GROUPS, LANES), lambda i: (i, 0)),
                   pl.BlockSpec((nseq, NSA_HEADS, LANES), per_b)),
        out_shape=(jax.ShapeDtypeStruct((b2 * KV_GROUPS, LANES), jnp.int32),
                   jax.ShapeDtypeStruct((b2, NSA_HEADS, LANES), _F32)),
        compiler_params=_params(("parallel",)),
        name="sample_topk",
    )(q3, kcb, vcb, cover_s)


def _s_attn_kernel(q_pos, nsel, past, wbuf, idx_ref, pt_ref, *refs):
    del pt_ref
    n_top = min(SEL_TOPN, nsel)
    nblk = KV_GROUPS * n_top
    blk_refs = refs[:nblk]
    q_ref, gates_ref, oc_ref, rows_ref, win_ref, cw_ref, o_ref, wout_ref = refs[nblk:]
    b = pl.program_id(0)
    q8 = _sample_queries(q_ref[...])
    row1 = lax.broadcasted_iota(jnp.int32, (NSA_HEADS, 1), 0)
    is_g0 = row1 < HPG

    nk = n_top * PAGE_SIZE
    lane_k = lax.broadcasted_iota(jnp.int32, (1, nk), 1)
    kslot = lane_k // PAGE_SIZE
    khalf = (lane_k % PAGE_SIZE) // SEL_BLK
    o_s = None
    for g in range(KV_GROUPS):
        kcat = jnp.concatenate([blk_refs[g * n_top + k][0] for k in range(n_top)], axis=1)
        vcat = jnp.concatenate([blk_refs[g * n_top + k][1] for k in range(n_top)], axis=1)
        s = _dot(q8, kcat)
        bias = jnp.full((1, nk), NEG_BIG, _F32)
        has_new = jnp.zeros((1, 1), _F32)
        for k in range(n_top):
            jk = idx_ref[(b * KV_GROUPS + g) * n_top + k]
            is_new = jk == nsel - 1
            half = jnp.where(is_new, -1, jk % 2)
            bias = jnp.where((kslot == k) & (khalf == half), 0.0, bias)
            has_new = jnp.where(is_new, 1.0, has_new)
        s = s + bias
        k_new = rows_ref[:, 2 * LANES:3 * LANES]
        v_new = rows_ref[:, 3 * LANES:4 * LANES]
        s_new = jnp.sum(q8 * k_new, axis=-1, keepdims=True) + jnp.where(has_new > 0.0, 0.0, NEG_BIG)
        m = jnp.maximum(jnp.max(s, axis=-1, keepdims=True), s_new)
        e = jnp.exp2(s - m)
        e_new = jnp.exp2(s_new - m)
        den = jnp.sum(e, axis=-1, keepdims=True) + e_new
        o_g = (_dot_nt(e, vcat) + e_new * v_new) / den
        o_s = o_g if o_s is None else jnp.where(is_g0, o_s, o_g)

    s_w = _dot(q8, cw_ref[0])
    wpos = (past - wbuf) + lax.broadcasted_iota(jnp.int32, (1, wbuf), 1)
    wmask = (wpos <= q_pos) & (wpos > q_pos - WINDOW) & (wpos >= 0)
    s_w = jnp.where(wmask, s_w, NEG_BIG)
    s_wn = jnp.sum(q8 * win_ref[:, 0:LANES], axis=-1, keepdims=True)
    m = jnp.maximum(jnp.max(s_w, axis=-1, keepdims=True), s_wn)
    e = jnp.exp2(s_w - m)
    e_new = jnp.exp2(s_wn - m)
    den = jnp.sum(e, axis=-1, keepdims=True) + e_new
    o_w = (_dot_nt(e, cw_ref[1]) + e_new * win_ref[:, LANES:2 * LANES]) / den

    gt = jnp.broadcast_to(gates_ref[...], (NSA_HEADS, LANES))
    lane = lax.broadcasted_iota(jnp.int32, (NSA_HEADS, LANES), 1)
    row = lax.broadcasted_iota(jnp.int32, (NSA_HEADS, LANES), 0)
    o = jnp.zeros((NSA_HEADS, LANES), _F32)
    for jb, ob in enumerate((oc_ref[...], o_s, o_w)):
        gcol = jnp.sum(jnp.where(lane == 3 * row + jb, gt, 0.0), axis=-1, keepdims=True)
        o = o + gcol * ob
    heads = [o[h:h + 1] for h in range(NSA_HEADS)]
    for k, s in enumerate(_place_heads(heads, 1)):
        o_ref[:, k * LANES:(k + 1) * LANES] = s.astype(o_ref.dtype)

    eye = (lax.broadcasted_iota(jnp.int32, (KV_W, LANES), 0) == lax.broadcasted_iota(jnp.int32, (KV_W, LANES), 1))
    tok = lax.broadcasted_iota(jnp.int32, (KV_W, wbuf), 1)
    for c in range(2):
        new_row = jnp.broadcast_to(win_ref[:, c * LANES:(c + 1) * LANES], (KV_W, LANES))
        new_col = jnp.sum(jnp.where(eye, new_row, 0.0), axis=-1, keepdims=True)
        wout_ref[c] = jnp.where(tok == wbuf - 1, new_col, pltpu.roll(cw_ref[c], wbuf - 1, 1))


def _s_attn(idx_flat, pt_flat, pool_t, q3, gates3, oc, rows3, win3, cache_win_t, q_pos, nsel, past, n_pages):
    b2 = q3.shape[0]
    wbuf = cache_win_t.shape[3]
    n_top = min(SEL_TOPN, nsel)
    per_b = lambda i, idx, pt: (i, 0, 0)

    def blk_spec(g, k):
        def imap(i, idx, pt):
            j = jnp.minimum(idx[(i * KV_GROUPS + g) * n_top + k], nsel - 2)
            return (pt[i * n_pages + j // 2], 1, 0, 0)
        return pl.BlockSpec((None, 2, KV_W, PAGE_SIZE), imap)

    grid_spec = pltpu.PrefetchScalarGridSpec(
        num_scalar_prefetch=2,
        grid=(b2,),
        in_specs=[blk_spec(g, k) for g in range(KV_GROUPS) for k in range(n_top)] + [
            pl.BlockSpec((None, 1, NSA_W), per_b),
            pl.BlockSpec((None, 1, LANES), per_b),
            pl.BlockSpec((None, NSA_HEADS, LANES), per_b),
            pl.BlockSpec((None, 1, 4 * KV_W), per_b),
            pl.BlockSpec((None, 1, 2 * KV_W), per_b),
            pl.BlockSpec((None, 2, KV_W, wbuf), lambda i, idx, pt: (i, 0, 0, 0)),
        ],
        out_specs=(pl.BlockSpec((None, 1, NSA_W), per_b),
                   pl.BlockSpec((None, 2, KV_W, wbuf), lambda i, idx, pt: (i, 0, 0, 0))),
    )
    return pl.pallas_call(
        functools.partial(_s_attn_kernel, q_pos, nsel, past, wbuf),
        grid_spec=grid_spec,
        out_shape=(jax.ShapeDtypeStruct((b2, 1, NSA_W), _F32),
                   jax.ShapeDtypeStruct((b2, 2, KV_W, wbuf), _F32)),
        compiler_params=_params(("arbitrary",)),
        name="sample_attn",
    )(idx_flat, pt_flat, *([pool_t] * (KV_GROUPS * n_top)), q3, gates3, oc, rows3, win3, cache_win_t)


def _s_ret_kernel(q_ref, k_ref, v_ref, g_ref, gam_ref, st_ref, e1_ref, e2_ref, o_ref, sn_ref):
    q, k, v = q_ref[...], k_ref[...], v_ref[...]
    gam = gam_ref[...]
    st = st_ref[...]
    q_exp = _dot(q, e1_ref[...])
    k_exp = _dot(k, e1_ref[...])
    v_exp = _dot(v, e2_ref[...])
    prod = q_exp * st.astype(_MXU).astype(_F32)
    qs = jnp.zeros(q.shape, _F32)
    for part in _split3(prod):
        qs = qs + lax.dot_general(part, e2_ref[...], (((1,), (1,)), ((), ())), preferred_element_type=_F32)
    qr = q.astype(_MXU).astype(_F32)
    kr = k.astype(_MXU).astype(_F32)
    att = jnp.sum(qr * kr, axis=-1, keepdims=True)
    o = att.astype(_MXU).astype(_F32) * v.astype(_MXU).astype(_F32) + qs * gam
    sn_ref[...] = st * gam + k_exp * v_exp
    ms = jnp.mean(o * o, axis=-1, keepdims=True)
    o_ref[...] = (o * lax.rsqrt(ms + RMS_EPS) * jax.nn.silu(g_ref[...])).astype(o_ref.dtype)


def _s_ret(q, k, v, g, gam, st2, e1, e2, tr):
    n = q.shape[0]
    dd = HEAD_DIM * HEAD_DIM
    row = lambda i: (i, 0)
    const = lambda i: (0, 0)
    return pl.pallas_call(
        _s_ret_kernel,
        grid=(n // tr,),
        in_specs=[
            pl.BlockSpec((tr, HEAD_DIM), row),
            pl.BlockSpec((tr, HEAD_DIM), row),
            pl.BlockSpec((tr, HEAD_DIM), row),
            pl.BlockSpec((tr, HEAD_DIM), row),
            pl.BlockSpec((tr, 1), row),
            pl.BlockSpec((tr, dd), row),
            pl.BlockSpec((HEAD_DIM, dd), const),
            pl.BlockSpec((HEAD_DIM, dd), const),
        ],
        out_specs=(pl.BlockSpec((tr, HEAD_DIM), row), pl.BlockSpec((tr, dd), row)),
        out_shape=(jax.ShapeDtypeStruct((n, HEAD_DIM), _MXU), jax.ShapeDtypeStruct((n, dd), _F32)),
        compiler_params=_params(("parallel",)),
        name="sample_ret",
    )(q, k, v, g, gam, st2, e1, e2)


def _rope_tables(pos):
    half = HEAD_DIM // 2
    inv = 1.0 / (ROPE_THETA ** (jnp.arange(half, dtype=_F32) / half))
    ang = pos.astype(_F32)[:, None] * inv[None, :]
    cos, sin = jnp.cos(ang), jnp.sin(ang)
    zero = jnp.zeros_like(sin)
    reps = LANES // HEAD_DIM
    cos_t = jnp.tile(cos, (1, 2 * reps))
    sa = jnp.tile(jnp.concatenate([-sin, zero], axis=1), (1, reps))
    sb = jnp.tile(jnp.concatenate([zero, sin], axis=1), (1, reps))
    return cos_t, sa, sb


def _cover_matrix(nsel, nb16, nsel_pad):
    c0 = np.arange(nb16) * CMP_STRIDE
    s0 = np.arange(nsel_pad) * SEL_BLK
    m = (c0[None, :] < s0[:, None] + SEL_BLK) & (c0[None, :] + CMP_BLK > s0[:, None])
    m = m & (np.arange(nsel_pad)[:, None] < nsel)
    return jnp.asarray(m.astype(np.float32), dtype=_MXU)


def _retention_tables():
    lg = jnp.log(1.0 - 2.0 ** (-5.0 - jnp.arange(RET_HEADS, dtype=_F32)))
    c = RET_CHUNK
    i = jnp.arange(c, dtype=_F32)
    diff = i[:, None] - i[None, :]
    causal = diff >= 0
    dmat = jnp.where(causal[None], jnp.exp(jnp.where(causal, diff, 0.0)[None] * lg[:, None, None]), 0.0)
    lane_head = jnp.arange(LANES) // HEAD_DIM
    pair_lg = lg.reshape(RET_HEADS // 2, 2)[:, lane_head]
    rowdec = jnp.exp((i + 1.0)[None, :, None] * pair_lg[:, None, :])
    kdec = jnp.exp((c - 1.0 - i)[None, :, None] * pair_lg[:, None, :])
    sdec = jnp.broadcast_to(jnp.exp(c * pair_lg)[:, :, None], (RET_HEADS // 2, LANES, LANES))
    bmask = (lane_head[:, None] == lane_head[None, :]).astype(_F32)
    return lg, (dmat, rowdec, kdec, sdec, bmask)


def _compress_weights(pos, w1, w2):
    w1t = w1.reshape(CMP_BLK, HEAD_DIM, CMP_HID)
    z1 = jnp.zeros_like(w1t)
    w1bd = jnp.concatenate([jnp.concatenate([w1t, z1], axis=2), jnp.concatenate([z1, w1t], axis=2)], axis=1)
    z2 = jnp.zeros_like(w2)
    w2bd = jnp.concatenate([jnp.concatenate([w2, z2], axis=1), jnp.concatenate([z2, w2], axis=1)], axis=0)
    w1pair = w1bd.reshape(CMP_BLK // 2, 2 * LANES, 2 * CMP_HID)
    return jnp.tile(pos, (1, 2)), w1pair.astype(_MXU), w2bd.astype(_MXU)


def kernel(x_prompt, x_sample, cache_kv, cache_win, state_ret, page_table, ln1, w_in, cmp_pos_k, cmp_w1_k,
           cmp_w2_k, cmp_pos_v, cmp_w1_v, cmp_w2_v, w_out, ln2, w_gate, w_up, w_down, ln_f):
    depth = ln1.shape[0]
    assert depth == 1, "single-layer step"
    b, t, d = x_prompt.shape
    b2, s_s, _ = x_sample.shape
    assert s_s == 1
    n_pages = page_table.shape[1]
    past = n_pages * PAGE_SIZE
    wbuf = cache_win.shape[2]
    assert t % (2 * Q_BLK) == 0 and t >= WINDOW + Q_BLK and (t // SEL_BLK) % 8 == 0
    l = 0

    w = w_in[l]
    o_kv = NSA_W
    o_gt = o_kv + 6 * KV_W
    o_r = o_gt + NSA_HEADS * 3
    w_perm = jnp.concatenate([
        w[:, :o_gt], w[:, o_r:], w[:, o_gt:o_r],
        jnp.zeros((d, LANES - NSA_HEADS * 3), w.dtype)], axis=1).astype(_MXU)
    pos_k, w1k, w2k = _compress_weights(cmp_pos_k[l], cmp_w1_k[l], cmp_w2_k[l])
    pos_v, w1v, w2v = _compress_weights(cmp_pos_v[l], cmp_w1_v[l], cmp_w2_v[l])
    cpos = jnp.stack([pos_k, pos_v])
    cw1 = jnp.stack([w1k, w1v])
    cw2 = jnp.stack([w2k, w2v])
    wo, wg, wu, wd = (a[l].astype(_MXU) for a in (w_out, w_gate, w_up, w_down))
    ln1r, ln2r, lnfr = ln1[l][None, :], ln2[l][None, :], ln_f[None, :]
    lg, ret_tabs = _retention_tables()

    tm = 512 if t % 512 == 0 else 256
    xp2 = x_prompt.reshape(b * t, d)
    q, _, _, rows_t, win_t, kk, vt, gates, ret = _proj(xp2, ln1r, w_perm, *_rope_tables(jnp.arange(t)), tm)
    kcb, vcbt = _compress_prompt(rows_t, cpos, cw1, cw2)
    nsel_p = t // SEL_BLK
    nb16_p = t // CMP_STRIDE
    eoh = (jnp.arange(t)[:, None] // SEL_BLK == jnp.arange(LANES)[None, :]).astype(_MXU)
    o_nsa = _nsa_prompt(q.reshape(b, t, NSA_W), gates.reshape(b, t, LANES), kcb, vcbt,
                        kk.reshape(b, t, 2 * KV_W), vt, eoh, _cover_matrix(nsel_p, nb16_p, nsel_p),
                        2 if b % 2 == 0 else 1).reshape(b * t, NSA_W)
    o_ret, st_p = _ret_prompt(ret.reshape(b, t, 4 * RET_W), ret_tabs, 4 if b % 4 == 0 else 1)
    o_ret = o_ret.reshape(b * t, RET_W)
    y_prompt = _post(xp2, o_nsa, o_ret, wo, ln2r, wg, wu, wd, lnfr, tm).reshape(b, t, d)
    to_cache = lambda a: jnp.transpose(a.reshape(a.shape[0], a.shape[1], KV_GROUPS, HEAD_DIM, a.shape[3]),
                                       (0, 4, 1, 2, 3))[None]
    kv_prompt = to_cache(rows_t)
    win_keep = min(WINDOW, t)
    win_prompt = to_cache(win_t[:, :, :, t - win_keep:])
    ret_prompt = st_p[None]

    xs2 = x_sample.reshape(b2, d)
    tms = min(tm, b2)
    pos_s = jnp.full((b2,), past, jnp.int32)
    q_s, rows_s, win_s, rows_st, _, _, _, gates_s, ret_s = _proj(xs2, ln1r, w_perm, *_rope_tables(pos_s), tms)
    n_pool = cache_kv.shape[1]
    pool_t = jnp.transpose(cache_kv[l], (0, 2, 3, 4, 1)).reshape(n_pool, 4, KV_W, PAGE_SIZE)
    cache_win_t = jnp.transpose(cache_win[l], (0, 2, 3, 4, 1)).reshape(b2, 2, KV_W, wbuf)
    pt_flat = page_table.reshape(-1).astype(jnp.int32)
    kcb_s, vcb_s = _compress_sample(pool_t, pt_flat, b2, n_pages, cpos, cw1, cw2)
    seq_len = past + 1
    nsel_s = -(-seq_len // SEL_BLK)
    nsel_pad = -(-nsel_s // LANES) * LANES
    nb16_s = past // CMP_STRIDE
    q3 = q_s.astype(_F32).reshape(b2, 1, NSA_W)
    idx8, oc = _s_topk(q3, kcb_s, vcb_s, _cover_matrix(nsel_s, nb16_s, nsel_pad), past, nsel_s)
    n_top = min(SEL_TOPN, nsel_s)
    idx_flat = idx8[:, :n_top].reshape(-1)
    o_nsa_s, win_new_t = _s_attn(idx_flat, pt_flat, pool_t, q3, gates_s.reshape(b2, 1, LANES), oc,
                                 rows_s.reshape(b2, 1, 4 * KV_W), win_s.reshape(b2, 1, 2 * KV_W), cache_win_t,
                                 past, nsel_s, past, n_pages)
    o_nsa_s = o_nsa_s.reshape(b2, NSA_W).astype(_MXU)
    nr = b2 * RET_HEADS
    heads = lambda a: a.reshape(nr, HEAD_DIM)
    rq, rk, rv, rg = (heads(ret_s[:, k * RET_W:(k + 1) * RET_W]) for k in range(4))
    gam = jnp.tile(jnp.exp(lg), (b2,))[:, None]
    dd = HEAD_DIM * HEAD_DIM
    lane_d = jnp.arange(dd) // HEAD_DIM
    lane_e = jnp.arange(dd) % HEAD_DIM
    e1 = (jnp.arange(HEAD_DIM)[:, None] == lane_d[None, :]).astype(_MXU)
    e2 = (jnp.arange(HEAD_DIM)[:, None] == lane_e[None, :]).astype(_MXU)
    o_ret_s, st_s = _s_ret(rq, rk, rv, rg, gam, state_ret[l].reshape(nr, dd), e1, e2, min(128, nr))
    y_sample = _post(xs2, o_nsa_s, o_ret_s.reshape(b2, RET_W), wo, ln2r, wg, wu, wd, lnfr, tms).reshape(b2, 1, d)
    kv_sample = jnp.transpose(rows_st.reshape(4, KV_GROUPS, HEAD_DIM, b2), (3, 0, 1, 2))[None, :, None]
    win_sample = to_cache(win_new_t)
    ret_sample = st_s.reshape(1, b2, RET_HEADS, HEAD_DIM, HEAD_DIM)
    return (y_prompt, y_sample, kv_prompt, kv_sample, win_prompt, win_sample, ret_prompt, ret_sample)
```

```python
import functools

import numpy as np
import jax
import jax.numpy as jnp
from jax import lax
from jax.experimental import pallas as pl
from jax.experimental.pallas import tpu as pltpu

HEAD_DIM = 64
NSA_HEADS = 8
RET_HEADS = 8
KV_GROUPS = 2
HPG = NSA_HEADS // KV_GROUPS
CMP_BLK = 32
CMP_STRIDE = 16
CMP_HID = 4 * HEAD_DIM
SEL_BLK = 64
SEL_TOPN = 16
WINDOW = 512
Q_BLK = 128
RET_CHUNK = 256
PAGE_SIZE = 128
ROPE_THETA = 10000.0
RMS_EPS = 1e-6

LANES = 128
NSA_W = NSA_HEADS * HEAD_DIM
RET_W = RET_HEADS * HEAD_DIM
KV_W = KV_GROUPS * HEAD_DIM
NEG_BIG = -(2.0 ** 100)
LOG2E = 1.4426950408889634
VT_ROWS = KV_W + 16
VMEM_LIMIT = 56 * 1024 * 1024

_MXU = jnp.bfloat16
_F32 = jnp.float32


def _dot(a, b):
    return jnp.dot(a.astype(_MXU), b.astype(_MXU), preferred_element_type=_F32)


def _dot_nt(a, b):
    return lax.dot_general(a.astype(_MXU), b.astype(_MXU), (((1,), (1,)), ((), ())),
                           preferred_element_type=_F32)


def _dot_tn(a, b):
    return lax.dot_general(a.astype(_MXU), b.astype(_MXU), (((0,), (0,)), ((), ())),
                           preferred_element_type=_F32)


def _split3(x):
    hi = x.astype(_MXU)
    r1 = x - hi.astype(_F32)
    mid = r1.astype(_MXU)
    lo = (r1 - mid.astype(_F32)).astype(_MXU)
    return hi, mid, lo


def _params(sem):
    return pltpu.CompilerParams(dimension_semantics=sem, vmem_limit_bytes=VMEM_LIMIT)


_C_Q, _C_KV, _C_RQ, _C_RK, _C_RV, _C_RG, _C_GT = 0, 512, 1280, 1792, 2304, 2816, 3328
_PROJ_COLS = 3456


def _proj_kernel(row_major, x_ref, g_ref, w_ref, cos_ref, sa_ref, sb_ref, q_ref, *out_refs):
    if row_major:
        rows_ref, win_ref = out_refs[:2]
        out_refs = out_refs[2:]
    rows_t_ref, win_t_ref, kk_ref, vt_ref, gates_ref, ret_ref = out_refs
    x = x_ref[...]
    ms = jnp.mean(x * x, axis=-1, keepdims=True)
    h = (x * lax.rsqrt(ms + RMS_EPS) * g_ref[...]).astype(_MXU)
    cos, sa, sb = cos_ref[...], sa_ref[...], sb_ref[...]

    def seg(c0, n):
        return jnp.dot(h, w_ref[:, c0:c0 + n], preferred_element_type=_F32)

    def rope(p):
        return p * cos + pltpu.roll(p, LANES - 32, 1) * sa + pltpu.roll(p, 32, 1) * sb

    def slab(p, s):
        return p[:, s * LANES:(s + 1) * LANES]

    scale = HEAD_DIM ** -0.5
    p = seg(_C_Q, NSA_W)
    for s in range(4):
        q_ref[:, s * LANES:(s + 1) * LANES] = (rope(slab(p, s)) * (scale * LOG2E)).astype(q_ref.dtype)
    p = seg(_C_KV, 6 * KV_W)
    kc, vc = rope(slab(p, 0)), slab(p, 1)
    ks, vs = rope(slab(p, 2)), slab(p, 3)
    kw, vw = rope(slab(p, 4)), slab(p, 5)
    if row_major:
        rows_ref[:, 0:128] = kc
        rows_ref[:, 128:256] = vc
        rows_ref[:, 256:384] = ks
        rows_ref[:, 384:512] = vs
        win_ref[:, 0:128] = kw
        win_ref[:, 128:256] = vw
    kk_ref[:, 0:128] = ks.astype(kk_ref.dtype)
    kk_ref[:, 128:256] = kw.astype(kk_ref.dtype)
    vs_t, vw_t = vs.T, vw.T
    rows_t_ref[0] = kc.T
    rows_t_ref[1] = vc.T
    rows_t_ref[2] = ks.T
    rows_t_ref[3] = vs_t
    win_t_ref[0] = kw.T
    win_t_ref[1] = vw_t
    ones = jnp.ones((VT_ROWS - KV_W, vs_t.shape[1]), vt_ref.dtype)
    vt_ref[0, 0:KV_W] = vs_t.astype(vt_ref.dtype)
    vt_ref[0, KV_W:VT_ROWS] = ones
    vt_ref[1, 0:KV_W] = vw_t.astype(vt_ref.dtype)
    vt_ref[1, KV_W:VT_ROWS] = ones
    p = seg(_C_RQ, RET_W)
    for s in range(4):
        ret_ref[:, s * LANES:(s + 1) * LANES] = rope(slab(p, s))
    p = seg(_C_RK, RET_W)
    for s in range(4):
        ret_ref[:, RET_W + s * LANES:RET_W + (s + 1) * LANES] = rope(slab(p, s)) * scale
    ret_ref[:, 2 * RET_W:3 * RET_W] = seg(_C_RV, RET_W)
    ret_ref[:, 3 * RET_W:4 * RET_W] = seg(_C_RG, RET_W)
    gates_ref[...] = jax.nn.sigmoid(seg(_C_GT, LANES))


def _proj(x2, ln, w_perm, cos, sa, sb, tm, row_major):
    n, d = x2.shape
    tt = cos.shape[0]
    nt = tt // tm
    row = lambda i: (i, 0)
    tab = lambda i: (i % nt, 0)
    const = lambda i: (0, 0)
    tok_t = lambda i: (i // nt, 0, 0, i % nt)
    outs = [
        (jax.ShapeDtypeStruct((n, NSA_W), _MXU), pl.BlockSpec((tm, NSA_W), row)),
    ]
    if row_major:
        outs += [
            (jax.ShapeDtypeStruct((n, 4 * KV_W), _F32), pl.BlockSpec((tm, 4 * KV_W), row)),
            (jax.ShapeDtypeStruct((n, 2 * KV_W), _F32), pl.BlockSpec((tm, 2 * KV_W), row)),
        ]
    outs += [
        (jax.ShapeDtypeStruct((n // tt, 4, KV_W, tt), _F32), pl.BlockSpec((None, 4, KV_W, tm), tok_t)),
        (jax.ShapeDtypeStruct((n // tt, 2, KV_W, tt), _F32), pl.BlockSpec((None, 2, KV_W, tm), tok_t)),
        (jax.ShapeDtypeStruct((n, 2 * KV_W), _MXU), pl.BlockSpec((tm, 2 * KV_W), row)),
        (jax.ShapeDtypeStruct((n // tt, 2, VT_ROWS, tt), _MXU), pl.BlockSpec((None, 2, VT_ROWS, tm), tok_t)),
        (jax.ShapeDtypeStruct((n, LANES), _F32), pl.BlockSpec((tm, LANES), row)),
        (jax.ShapeDtypeStruct((n, 4 * RET_W), _F32), pl.BlockSpec((tm, 4 * RET_W), row)),
    ]
    return pl.pallas_call(
        functools.partial(_proj_kernel, row_major),
        grid=(n // tm,),
        in_specs=[
            pl.BlockSpec((tm, d), row),
            pl.BlockSpec((1, d), const),
            pl.BlockSpec((d, _PROJ_COLS), const),
            pl.BlockSpec((tm, LANES), tab),
            pl.BlockSpec((tm, LANES), tab),
            pl.BlockSpec((tm, LANES), tab),
        ],
        out_specs=tuple(spec for _, spec in outs),
        out_shape=tuple(shape for shape, _ in outs),
        compiler_params=_params(("parallel",)),
        name="proj",
    )(x2, ln, w_perm, cos, sa, sb)


def _post_kernel(x_ref, on_ref, or_ref, wo_ref, g2_ref, wg_ref, wu_ref, wd_ref, gf_ref, y_ref):
    x = x_ref[...]
    mix = (jnp.dot(on_ref[...], wo_ref[0:NSA_W, :], preferred_element_type=_F32)
           + jnp.dot(or_ref[...], wo_ref[NSA_W:NSA_W + RET_W, :], preferred_element_type=_F32))
    x1 = x + mix
    ms = jnp.mean(x1 * x1, axis=-1, keepdims=True)
    h = (x1 * lax.rsqrt(ms + RMS_EPS) * g2_ref[...]).astype(_MXU)
    a = jax.nn.silu(jnp.dot(h, wg_ref[...], preferred_element_type=_F32))
    a = a * jnp.dot(h, wu_ref[...], preferred_element_type=_F32)
    y = x1 + jnp.dot(a.astype(_MXU), wd_ref[...], preferred_element_type=_F32)
    ms = jnp.mean(y * y, axis=-1, keepdims=True)
    y_ref[...] = y * lax.rsqrt(ms + RMS_EPS) * gf_ref[...]


def _post(x2, o_nsa, o_ret, w_out, ln2, w_gate, w_up, w_down, ln_f, tm):
    n, d = x2.shape
    dff = w_gate.shape[1]
    row = lambda i: (i, 0)
    const = lambda i: (0, 0)
    once = dict(pipeline_mode=pl.Buffered(1))
    return pl.pallas_call(
        _post_kernel,
        grid=(n // tm,),
        in_specs=[
            pl.BlockSpec((tm, d), row),
            pl.BlockSpec((tm, NSA_W), row),
            pl.BlockSpec((tm, RET_W), row),
            pl.BlockSpec((d, d), const, **once),
            pl.BlockSpec((1, d), const),
            pl.BlockSpec((d, dff), const, **once),
            pl.BlockSpec((d, dff), const, **once),
            pl.BlockSpec((dff, d), const, **once),
            pl.BlockSpec((1, d), const),
        ],
        out_specs=pl.BlockSpec((tm, d), row),
        out_shape=jax.ShapeDtypeStruct((n, d), _F32),
        compiler_params=_params(("parallel",)),
        name="post",
    )(x2, o_nsa, o_ret, w_out, ln2, w_gate, w_up, w_down, ln_f)


def _fill_token_rows(planes, buf_ref, c):
    for j, plane in enumerate(planes):
        buf_ref[c, j * LANES:(j + 1) * LANES, :] = plane.T


def _compress_rows(buf_ref, c, nb16, pos_ref, w1_ref, w2_ref):
    half = CMP_STRIDE // 2
    a = jnp.zeros((nb16, 2 * CMP_HID), _F32)
    b = jnp.zeros((nb16, 2 * CMP_HID), _F32)
    for tp in range(half):
        t0, t1 = 2 * tp, 2 * tp + 1
        x0 = buf_ref[c, pl.ds(t0, nb16, stride=CMP_STRIDE), :]
        x1 = buf_ref[c, pl.ds(t1, nb16, stride=CMP_STRIDE), :]
        xa = jnp.concatenate([x0 + pos_ref[c, t0:t0 + 1, :], x1 + pos_ref[c, t1:t1 + 1, :]], axis=1)
        xb = jnp.concatenate([x0 + pos_ref[c, CMP_STRIDE + t0:CMP_STRIDE + t0 + 1, :],
                              x1 + pos_ref[c, CMP_STRIDE + t1:CMP_STRIDE + t1 + 1, :]], axis=1)
        a = a + _dot(xa, w1_ref[c, tp])
        b = b + _dot(xb, w1_ref[c, half + tp])
    hid = a + pltpu.roll(b, nb16 - 1, 0)
    return _dot(jax.nn.gelu(hid), w2_ref[c])


def _compress_prompt_kernel(rt_ref, pos_ref, w1_ref, w2_ref, kcb_ref, vcbt_ref, buf_ref):
    nb16 = kcb_ref.shape[0]
    outs = []
    for c in range(2):
        planes = [rt_ref[c, :, j * LANES:(j + 1) * LANES] for j in range(nb16 * CMP_STRIDE // LANES)]
        _fill_token_rows(planes, buf_ref, c)
        outs.append(_compress_rows(buf_ref, c, nb16, pos_ref, w1_ref, w2_ref))
    kcb_ref[...] = outs[0]
    vcbt_ref[...] = outs[1].T


def _compress_prompt(rows_t, pos, w1bd, w2bd):
    b, _, _, t = rows_t.shape
    nb16 = t // CMP_STRIDE
    c3 = lambda i: (0, 0, 0)
    c4 = lambda i: (0, 0, 0, 0)
    out = jax.ShapeDtypeStruct((b, nb16, LANES), _F32)
    return pl.pallas_call(
        _compress_prompt_kernel,
        grid=(b,),
        in_specs=[
            pl.BlockSpec((None, 2, KV_W, t), lambda i: (i, 0, 0, 0)),
            pl.BlockSpec(pos.shape, c3),
            pl.BlockSpec(w1bd.shape, c4),
            pl.BlockSpec(w2bd.shape, c3),
        ],
        out_specs=(pl.BlockSpec((None, nb16, LANES), lambda i: (i, 0, 0)),
                   pl.BlockSpec((None, LANES, nb16), lambda i: (i, 0, 0))),
        out_shape=(out, jax.ShapeDtypeStruct((b, LANES, nb16), _F32)),
        scratch_shapes=[pltpu.VMEM((2, t, LANES), _F32)],
        compiler_params=_params(("parallel",)),
        name="compress_prompt",
    )(rows_t, pos, w1bd, w2bd)


def _compress_sample_kernel(n_pages, pt_ref, *refs):
    page_refs = refs[:n_pages]
    pos_ref, w1_ref, w2_ref, kcb_ref, vcb_ref, buf_ref = refs[n_pages:]
    nb16 = kcb_ref.shape[0]
    for c, out_ref in enumerate((kcb_ref, vcb_ref)):
        _fill_token_rows([page_refs[j][c] for j in range(n_pages)], buf_ref, c)
        out_ref[...] = _compress_rows(buf_ref, c, nb16, pos_ref, w1_ref, w2_ref)


def _compress_sample(pool_t, page_table_flat, b2, n_pages, pos, w1bd, w2bd):
    past = n_pages * PAGE_SIZE
    nb16 = past // CMP_STRIDE
    c3 = lambda i, pt: (0, 0, 0)
    c4 = lambda i, pt: (0, 0, 0, 0)

    def page_spec(j):
        return pl.BlockSpec((None, 2, KV_W, PAGE_SIZE), lambda i, pt: (pt[i * n_pages + j], 0, 0, 0))

    out = jax.ShapeDtypeStruct((b2, nb16, LANES), _F32)
    grid_spec = pltpu.PrefetchScalarGridSpec(
        num_scalar_prefetch=1,
        grid=(b2,),
        in_specs=[page_spec(j) for j in range(n_pages)] + [
            pl.BlockSpec(pos.shape, c3),
            pl.BlockSpec(w1bd.shape, c4),
            pl.BlockSpec(w2bd.shape, c3),
        ],
        out_specs=(pl.BlockSpec((None, nb16, LANES), lambda i, pt: (i, 0, 0)),
                   pl.BlockSpec((None, nb16, LANES), lambda i, pt: (i, 0, 0))),
        scratch_shapes=[pltpu.VMEM((2, past, LANES), _F32)],
    )
    return pl.pallas_call(
        functools.partial(_compress_sample_kernel, n_pages),
        grid_spec=grid_spec,
        out_shape=(out, out),
        compiler_params=_params(("arbitrary",)),
        name="compress_sample",
    )(page_table_flat, *([pool_t] * n_pages), pos, w1bd, w2bd)


def _group_queries(qf, rows):
    lane = lax.broadcasted_iota(jnp.int32, (rows, LANES), 1)
    out = []
    for h in range(NSA_HEADS):
        g = h // HPG
        s = qf[:, (h // 2) * LANES:(h // 2 + 1) * LANES]
        if (h % 2) != g:
            s = pltpu.roll(s, HEAD_DIM, 1)
        out.append(jnp.where((lane >= g * HEAD_DIM) & (lane < (g + 1) * HEAD_DIM), s, 0.0))
    return out


def _masked_softmax(s, mask):
    s = jnp.where(mask, s, -jnp.inf)
    m = jnp.max(s, axis=-1, keepdims=True)
    m = jnp.where(m > -jnp.inf, m, 0.0)
    e = jnp.where(mask, jnp.exp2(s - m), 0.0)
    return e / jnp.maximum(jnp.sum(e, axis=-1, keepdims=True), 1e-30)


def _place_heads(o_heads, rows):
    lane = lax.broadcasted_iota(jnp.int32, (rows, LANES), 1)
    slabs = []
    for k in range(NSA_HEADS // 2):
        pair = []
        for h in (2 * k, 2 * k + 1):
            g = h // HPG
            o = o_heads[h]
            if (h % 2) != g:
                o = pltpu.roll(o, HEAD_DIM, 1)
            pair.append(o)
        slabs.append(jnp.where(lane < HEAD_DIM, pair[0], pair[1]))
    return slabs


def _softmax_keys(s):
    m = jnp.max(s, axis=0, keepdims=True)
    m = jnp.where(m > -jnp.inf, m, 0.0)
    e = jnp.exp2(s - m)
    return e * (1.0 / jnp.maximum(jnp.sum(e, axis=0, keepdims=True), 1e-30))


def _weights_keys(s):
    m = jnp.max(s, axis=0, keepdims=True)
    m = jnp.where(m > -jnp.inf, m, 0.0)
    return jnp.exp2((s - m).astype(_MXU))


def _normalise(o_aug):
    return o_aug[0:KV_W] * (1.0 / jnp.maximum(o_aug[KV_W:KV_W + 1], 1e-30))


def _nsa_prompt_kernel(nsel, q_ref, gates_ref, kcb_ref, vcbt_ref, kk_ref, vt_ref, eoh_ref, cover_ref,
                       o_ref, acc_ref, m_ref, s_ref, qa_ref):
    i = pl.program_id(1)
    nseq = q_ref.shape[0]
    seqs = range(nseq)
    tq = Q_BLK
    qpos_1 = i * tq + lax.broadcasted_iota(jnp.int32, (1, tq), 1)

    def per_head(x):
        return jnp.concatenate([x] * NSA_HEADS, axis=1)

    nb16 = kcb_ref.shape[1]
    blk_end = lax.broadcasted_iota(jnp.int32, (nb16, 1), 0) * CMP_STRIDE + (CMP_BLK - 1)
    bias_c = per_head(jnp.where(blk_end <= qpos_1, 0.0, -jnp.inf))
    q2 = [jnp.concatenate(_group_queries(q_ref[sq].astype(_F32), tq), axis=0).astype(_MXU)
          for sq in seqs]
    p_c = [_softmax_keys(_dot_nt(kcb_ref[sq], q2[sq]) + bias_c) for sq in seqs]
    o_c = [_dot(vcbt_ref[sq], p_c[sq]) for sq in seqs]

    jidx = lax.broadcasted_iota(jnp.int32, (nsel, tq), 0)
    qpos_l = i * tq + lax.broadcasted_iota(jnp.int32, (nsel, tq), 1)
    cur = qpos_l // SEL_BLK
    forced = (jidx == 0) | (jidx == cur) | (jidx == cur - 1)
    valid = jidx * SEL_BLK <= qpos_l
    n_top = min(SEL_TOPN, nsel)
    sub = lax.broadcasted_iota(jnp.int32, (8, tq), 0)
    nv = nsel // 8

    def choice_bias(p_cs, g):
        psum = p_cs[:, (g * HPG) * tq:(g * HPG + 1) * tq]
        for hh in range(1, HPG):
            psum = psum + p_cs[:, (g * HPG + hh) * tq:(g * HPG + hh + 1) * tq]
        imp_t = jnp.zeros((nsel, tq), _F32)
        for part in _split3(psum):
            imp_t = imp_t + jnp.dot(cover_ref[...], part, preferred_element_type=_F32)
        score = jnp.where(forced, jnp.inf, jnp.where(valid, imp_t, -jnp.inf))
        sc_v = [score[8 * v:8 * v + 8] for v in range(nv)]
        rank_v = [jnp.zeros((8, tq), jnp.int32) for _ in range(nv)]
        for jp in range(nsel):
            rowb = jnp.broadcast_to(score[jp:jp + 1, :], (8, tq))
            for v in range(nv):
                if v > jp // 8:
                    beats = jnp.where(rowb >= sc_v[v], 1, 0)
                elif v < jp // 8:
                    beats = jnp.where(rowb > sc_v[v], 1, 0)
                else:
                    beats = jnp.where(sub > (jp % 8), jnp.where(rowb >= sc_v[v], 1, 0),
                                      jnp.where(rowb > sc_v[v], 1, 0))
                rank_v[v] = rank_v[v] + beats
        sel_t = jnp.concatenate([jnp.where(r < n_top, 1.0, 0.0) for r in rank_v], axis=0)
        if nsel < LANES:
            sel_t = jnp.concatenate([sel_t, jnp.ones((LANES - nsel, tq), _F32)], axis=0)
        return ((sel_t.T - 1.0) * (-NEG_BIG)).astype(_MXU)

    for sq in seqs:
        qa_ref[sq, :, 0:LANES] = q2[sq]
        for g in range(KV_GROUPS):
            bias = choice_bias(p_c[sq], g)
            for hh in range(HPG):
                h = g * HPG + hh
                qa_ref[sq, h * tq:(h + 1) * tq, LANES:2 * LANES] = bias

    tk = 2 * tq
    wcb = 2 * tq
    ncb = NSA_HEADS * tq // wcb
    acc_ref[...] = jnp.zeros(acc_ref.shape, _F32)
    m_ref[...] = jnp.full(m_ref.shape, 2.0 * NEG_BIG, _F32)

    def scores(slot, kt):
        k0 = pl.multiple_of(kt * tk, tk)
        eoh = eoh_ref[pl.ds(k0, tk), :]
        for sq in seqs:
            k_aug = jnp.concatenate([kk_ref[sq, pl.ds(k0, tk), 0:LANES], eoh], axis=1)
            for cb in range(ncb):
                cs = slice(cb * wcb, (cb + 1) * wcb)
                s_ref[sq, slot, :, cs] = _dot_nt(k_aug, qa_ref[sq, cs, :])

    def consume(slot, kt, causal):
        k0 = pl.multiple_of(kt * tk, tk)
        if causal:
            kpos = k0 + lax.broadcasted_iota(jnp.int32, (tk, 1), 0)
            cbias = jnp.concatenate([jnp.where(kpos <= qpos_1, 0.0, NEG_BIG)] * (wcb // tq), axis=1)
        for sq in seqs:
            vt = vt_ref[sq, 0, :, pl.ds(k0, tk)]
            for cb in range(ncb):
                cs = slice(cb * wcb, (cb + 1) * wcb)
                s = s_ref[sq, slot, :, cs]
                if causal:
                    s = s + cbias
                m_old = m_ref[sq, :, cs]
                m_new = jnp.maximum(m_old, jnp.max(s, axis=0, keepdims=True))
                alpha = jnp.exp2(m_old - m_new)
                p = jnp.exp2((s - m_new).astype(_MXU))
                m_ref[sq, :, cs] = m_new
                acc_ref[sq, :, cs] = alpha * acc_ref[sq, :, cs] + _dot(vt, p)

    last = (i + 2) // 2 - 1
    scores(0, 0)

    def tile_pair(j, carry):
        scores(1, 2 * j + 1)
        consume(0, 2 * j, False)
        scores(0, 2 * j + 2)
        consume(1, 2 * j + 1, False)
        return carry

    lax.fori_loop(0, last // 2, tile_pair, 0)

    @pl.when(last % 2 == 1)
    def _():
        scores(1, last)
        consume(0, last - 1, False)
        consume(1, last, True)

    @pl.when(last % 2 == 0)
    def _():
        consume(0, last, True)

    wk = WINDOW + tq
    start = pl.multiple_of(jnp.maximum(i * tq - WINDOW, 0), tq)
    wpos = start + lax.broadcasted_iota(jnp.int32, (wk, 1), 0)
    bias_w = per_head(jnp.where(wpos <= qpos_1, jnp.where(wpos > qpos_1 - WINDOW, 0.0, -jnp.inf), -jnp.inf))
    for sq in seqs:
        o_s = _normalise(acc_ref[sq])
        e_w = _weights_keys(_dot_nt(kk_ref[sq, pl.ds(start, wk), LANES:2 * LANES], q2[sq]) + bias_w)
        o_w = _normalise(_dot(vt_ref[sq, 1, :, pl.ds(start, wk)], e_w))

        gt_t = gates_ref[sq].T
        o_rows = []
        for h in range(NSA_HEADS):
            g = h // HPG
            rs = slice(g * HEAD_DIM, (g + 1) * HEAD_DIM)
            cs = slice(h * tq, (h + 1) * tq)
            o_rows.append(gt_t[3 * h:3 * h + 1] * o_c[sq][rs, cs] + gt_t[3 * h + 1:3 * h + 2] * o_s[rs, cs]
                          + gt_t[3 * h + 2:3 * h + 3] * o_w[rs, cs])
        o_ref[sq] = jnp.concatenate(o_rows, axis=0).T.astype(o_ref.dtype)


def _nsa_prompt(q3, gates3, kcb, vcbt, kk3, vt4, eoh, cover, nseq):
    b, t, _ = q3.shape
    nq = t // Q_BLK
    nb16 = t // CMP_STRIDE
    nsel = t // SEL_BLK
    qblk = lambda bi, i: (bi, i, 0)
    per_b = lambda bi, i: (bi, 0, 0)
    const = lambda bi, i: (0, 0)
    ncol = NSA_HEADS * Q_BLK
    return pl.pallas_call(
        functools.partial(_nsa_prompt_kernel, nsel),
        grid=(b // nseq, nq),
        in_specs=[
            pl.BlockSpec((nseq, Q_BLK, NSA_W), qblk),
            pl.BlockSpec((nseq, Q_BLK, LANES), qblk),
            pl.BlockSpec((nseq, nb16, LANES), per_b),
            pl.BlockSpec((nseq, LANES, nb16), per_b),
            pl.BlockSpec((nseq, t, 2 * KV_W), per_b),
            pl.BlockSpec((nseq, 2, VT_ROWS, t), lambda bi, i: (bi, 0, 0, 0)),
            pl.BlockSpec((t, LANES), const),
            pl.BlockSpec((nsel, nb16), const),
        ],
        out_specs=pl.BlockSpec((nseq, Q_BLK, NSA_W), qblk),
        out_shape=jax.ShapeDtypeStruct((b, t, NSA_W), _MXU),
        scratch_shapes=[pltpu.VMEM((nseq, VT_ROWS, ncol), _F32),
                        pltpu.VMEM((nseq, 1, ncol), _F32),
                        pltpu.VMEM((nseq, 2, 2 * Q_BLK, ncol), _F32),
                        pltpu.VMEM((nseq, ncol, 2 * LANES), _MXU)],
        compiler_params=_params(("parallel", "arbitrary")),
        name="nsa_prompt",
    )(q3, gates3, kcb, vcbt, kk3, vt4, eoh, cover)


def _ret_prompt_kernel(ret_ref, dmat_ref, rowdec_ref, kdec_ref, sdec_ref, bmask_ref,
                       o_ref, st_ref, s_ref):
    c = pl.program_id(1)
    nc = pl.num_programs(1)
    nseq = ret_ref.shape[0]
    tq = RET_CHUNK

    @pl.when(c == 0)
    def _():
        s_ref[...] = jnp.zeros(s_ref.shape, _F32)

    lane = lax.broadcasted_iota(jnp.int32, (tq, LANES), 1)
    lo = lane < HEAD_DIM
    bmask = bmask_ref[...]
    for pr in range(RET_HEADS // 2):
        c0 = pr * LANES
        dm = jnp.concatenate([dmat_ref[2 * pr], dmat_ref[2 * pr + 1]], axis=0)
        for sq in range(nseq):
            q = ret_ref[sq, :, c0:c0 + LANES]
            k = ret_ref[sq, :, RET_W + c0:RET_W + c0 + LANES]
            v = ret_ref[sq, :, 2 * RET_W + c0:2 * RET_W + c0 + LANES]
            g = ret_ref[sq, :, 3 * RET_W + c0:3 * RET_W + c0 + LANES]
            q2 = jnp.concatenate([jnp.where(lo, q, 0.0), jnp.where(lo, 0.0, q)], axis=0)
            o2 = _dot(_dot_nt(q2, k) * dm, v)
            o = jnp.where(lo, o2[0:tq], o2[tq:2 * tq])
            s_old = s_ref[sq, pr]
            o = o + _dot(q, s_old) * rowdec_ref[pr]
            s_ref[sq, pr] = s_old * sdec_ref[pr] + _dot_tn(k * kdec_ref[pr], v) * bmask
            o_sq = o * o
            s0 = jnp.sum(jnp.where(lo, o_sq, 0.0), axis=-1, keepdims=True)
            s1 = jnp.sum(jnp.where(lo, 0.0, o_sq), axis=-1, keepdims=True)
            ms = jnp.where(lo, s0, s1) * (1.0 / HEAD_DIM)
            o = o * lax.rsqrt(ms + RMS_EPS) * jax.nn.silu(g)
            o_ref[sq, :, c0:c0 + LANES] = o.astype(o_ref.dtype)

    @pl.when(c == nc - 1)
    def _():
        for sq in range(nseq):
            for pr in range(RET_HEADS // 2):
                s_fin = s_ref[sq, pr]
                st_ref[sq, 2 * pr] = s_fin[0:HEAD_DIM, 0:HEAD_DIM]
                st_ref[sq, 2 * pr + 1] = pltpu.roll(s_fin, HEAD_DIM, 1)[HEAD_DIM:2 * HEAD_DIM, 0:HEAD_DIM]


def _ret_prompt(ret3, tabs, nseq):
    b, t, _ = ret3.shape
    nc = t // RET_CHUNK
    blk = lambda bi, i: (bi, i, 0)
    c3 = lambda bi, i: (0, 0, 0)
    dmat, rowdec, kdec, sdec, bmask = tabs
    return pl.pallas_call(
        _ret_prompt_kernel,
        grid=(b // nseq, nc),
        in_specs=[
            pl.BlockSpec((nseq, RET_CHUNK, 4 * RET_W), blk),
            pl.BlockSpec(dmat.shape, c3),
            pl.BlockSpec(rowdec.shape, c3),
            pl.BlockSpec(kdec.shape, c3),
            pl.BlockSpec(sdec.shape, c3),
            pl.BlockSpec(bmask.shape, lambda bi, i: (0, 0)),
        ],
        out_specs=(pl.BlockSpec((nseq, RET_CHUNK, RET_W), blk),
                   pl.BlockSpec((nseq, RET_HEADS, HEAD_DIM, HEAD_DIM), lambda bi, i: (bi, 0, 0, 0))),
        out_shape=(jax.ShapeDtypeStruct((b, t, RET_W), _MXU),
                   jax.ShapeDtypeStruct((b, RET_HEADS, HEAD_DIM, HEAD_DIM), _F32)),
        scratch_shapes=[pltpu.VMEM((nseq, RET_HEADS // 2, LANES, LANES), _F32)],
        compiler_params=_params(("parallel", "arbitrary")),
        name="ret_prompt",
    )(ret3, dmat, rowdec, kdec, sdec, bmask)


def _sample_queries(q_row):
    heads = _group_queries(q_row.astype(_F32), 1)
    row = lax.broadcasted_iota(jnp.int32, (NSA_HEADS, LANES), 0)
    q8 = jnp.zeros((NSA_HEADS, LANES), _F32)
    for h in range(NSA_HEADS):
        q8 = jnp.where(row == h, jnp.broadcast_to(heads[h], (NSA_HEADS, LANES)), q8)
    return q8


def _s_topk_kernel(q_pos, nsel, q_ref, kcb_ref, vcb_ref, cover_ref, idx_ref, oc_ref):
    nseq, nb16, _ = kcb_ref.shape
    nrow = nseq * KV_GROUPS
    blk_end = lax.broadcasted_iota(jnp.int32, (NSA_HEADS, nb16), 1) * CMP_STRIDE + (CMP_BLK - 1)
    row = lax.broadcasted_iota(jnp.int32, (nrow, nb16), 0)
    psum = jnp.zeros((nrow, nb16), _F32)
    for s in range(nseq):
        q8 = _sample_queries(q_ref[s])
        p_c = _masked_softmax(_dot_nt(q8, kcb_ref[s]), blk_end <= q_pos)
        oc_ref[s] = _dot(p_c, vcb_ref[s])
        for g in range(KV_GROUPS):
            acc = p_c[g * HPG:g * HPG + 1]
            for hh in range(1, HPG):
                acc = acc + p_c[g * HPG + hh:g * HPG + hh + 1]
            psum = jnp.where(row == KV_GROUPS * s + g, jnp.broadcast_to(acc, (nrow, nb16)), psum)

    npad = cover_ref.shape[0]
    imp = jnp.zeros((nrow, npad), _F32)
    for part in _split3(psum):
        imp = imp + lax.dot_general(part, cover_ref[...], (((1,), (1,)), ((), ())),
                                    preferred_element_type=_F32)
    j = lax.broadcasted_iota(jnp.int32, (nrow, npad), 1)
    cur = q_pos // SEL_BLK
    forced = (j == 0) | (j == cur) | (j == cur - 1)
    valid = j * SEL_BLK <= q_pos
    score = jnp.where(forced, jnp.inf, jnp.where(valid, imp, -jnp.inf))
    jf = j.astype(_F32)
    alive = jnp.where(j < nsel, 1.0, 0.0)
    lane = lax.broadcasted_iota(jnp.int32, (nrow, LANES), 1)
    idx = jnp.zeros((nrow, LANES), _F32)
    for r in range(min(SEL_TOPN, nsel)):
        live = alive > 0.0
        m = jnp.max(jnp.where(live, score, -jnp.inf), axis=-1, keepdims=True)
        cand = jnp.where(live, jnp.where(score == m, jf, float(npad)), float(npad))
        jmin = jnp.min(cand, axis=-1, keepdims=True)
        idx = jnp.where(lane == r, jmin, idx)
        alive = jnp.where(jf == jmin, 0.0, alive)
    idx_ref[...] = idx.astype(jnp.int32)


def _s_topk(q3, kcb, vcb, cover_s, q_pos, nsel):
    b2 = q3.shape[0]
    nb16 = kcb.shape[1]
    nseq = 16 if b2 % 16 == 0 else (8 if b2 % 8 == 0 else b2)
    per_b = lambda i: (i, 0, 0)
    return pl.pallas_call(
        functools.partial(_s_topk_kernel, q_pos, nsel),
        grid=(b2 // nseq,),
        in_specs=[
            pl.BlockSpec((nseq, 1, NSA_W), per_b),
            pl.BlockSpec((nseq, nb16, LANES), per_b),
            pl.BlockSpec((nseq, nb16, LANES), per_b),
            pl.BlockSpec(cover_s.shape, lambda i: (0, 0)),
        ],
        out_specs=(pl.BlockSpec((nseq * KV_GROUPS, LANES), lambda i: (i, 0)),
                   pl.BlockSpec((nseq, NSA_HEADS, LANES), per_b)),
        out_shape=(jax.ShapeDtypeStruct((b2 * KV_GROUPS, LANES), jnp.int32),
                   jax.ShapeDtypeStruct((b2, NSA_HEADS, LANES), _F32)),
        compiler_params=_params(("parallel",)),
        name="sample_topk",
    )(q3, kcb, vcb, cover_s)


def _s_attn_kernel(q_pos, nsel, past, wbuf, idx_ref, pt_ref, *refs):
    del pt_ref
    n_top = min(SEL_TOPN, nsel)
    nblk = KV_GROUPS * n_top
    blk_refs = refs[:nblk]
    q_ref, gates_ref, oc_ref, rows_ref, win_ref, cw_ref, o_ref, wout_ref = refs[nblk:]
    b = pl.program_id(0)
    q8 = _sample_queries(q_ref[...])
    row1 = lax.broadcasted_iota(jnp.int32, (NSA_HEADS, 1), 0)
    is_g0 = row1 < HPG

    nk = n_top * PAGE_SIZE
    lane_k = lax.broadcasted_iota(jnp.int32, (1, nk), 1)
    kslot = lane_k // PAGE_SIZE
    khalf = (lane_k % PAGE_SIZE) // SEL_BLK
    o_s = None
    for g in range(KV_GROUPS):
        kcat = jnp.concatenate([blk_refs[g * n_top + k][0] for k in range(n_top)], axis=1)
        vcat = jnp.concatenate([blk_refs[g * n_top + k][1] for k in range(n_top)], axis=1)
        s = _dot(q8, kcat)
        bias = jnp.full((1, nk), NEG_BIG, _F32)
        has_new = jnp.zeros((1, 1), _F32)
        for k in range(n_top):
            jk = idx_ref[(b * KV_GROUPS + g) * n_top + k]
            is_new = jk == nsel - 1
            half = jnp.where(is_new, -1, jk % 2)
            bias = jnp.where((kslot == k) & (khalf == half), 0.0, bias)
            has_new = jnp.where(is_new, 1.0, has_new)
        s = s + bias
        k_new = rows_ref[:, 2 * LANES:3 * LANES]
        v_new = rows_ref[:, 3 * LANES:4 * LANES]
        s_new = jnp.sum(q8 * k_new, axis=-1, keepdims=True) + jnp.where(has_new > 0.0, 0.0, NEG_BIG)
        m = jnp.maximum(jnp.max(s, axis=-1, keepdims=True), s_new)
        e = jnp.exp2(s - m)
        e_new = jnp.exp2(s_new - m)
        den = jnp.sum(e, axis=-1, keepdims=True) + e_new
        o_g = (_dot_nt(e, vcat) + e_new * v_new) / den
        o_s = o_g if o_s is None else jnp.where(is_g0, o_s, o_g)

    s_w = _dot(q8, cw_ref[0])
    wpos = (past - wbuf) + lax.broadcasted_iota(jnp.int32, (1, wbuf), 1)
    wmask = (wpos <= q_pos) & (wpos > q_pos - WINDOW) & (wpos >= 0)
    s_w = jnp.where(wmask, s_w, NEG_BIG)
    s_wn = jnp.sum(q8 * win_ref[:, 0:LANES], axis=-1, keepdims=True)
    m = jnp.maximum(jnp.max(s_w, axis=-1, keepdims=True), s_wn)
    e = jnp.exp2(s_w - m)
    e_new = jnp.exp2(s_wn - m)
    den = jnp.sum(e, axis=-1, keepdims=True) + e_new
    o_w = (_dot_nt(e, cw_ref[1]) + e_new * win_ref[:, LANES:2 * LANES]) / den

    gt = jnp.broadcast_to(gates_ref[...], (NSA_HEADS, LANES))
    lane = lax.broadcasted_iota(jnp.int32, (NSA_HEADS, LANES), 1)
    row = lax.broadcasted_iota(jnp.int32, (NSA_HEADS, LANES), 0)
    o = jnp.zeros((NSA_HEADS, LANES), _F32)
    for jb, ob in enumerate((oc_ref[...], o_s, o_w)):
        gcol = jnp.sum(jnp.where(lane == 3 * row + jb, gt, 0.0), axis=-1, keepdims=True)
        o = o + gcol * ob
    heads = [o[h:h + 1] for h in range(NSA_HEADS)]
    for k, s in enumerate(_place_heads(heads, 1)):
        o_ref[:, k * LANES:(k + 1) * LANES] = s.astype(o_ref.dtype)

    eye = (lax.broadcasted_iota(jnp.int32, (KV_W, LANES), 0) == lax.broadcasted_iota(jnp.int32, (KV_W, LANES), 1))
    tok = lax.broadcasted_iota(jnp.int32, (KV_W, wbuf), 1)
    for c in range(2):
        new_row = jnp.broadcast_to(win_ref[:, c * LANES:(c + 1) * LANES], (KV_W, LANES))
        new_col = jnp.sum(jnp.where(eye, new_row, 0.0), axis=-1, keepdims=True)
        wout_ref[c] = jnp.where(tok == wbuf - 1, new_col, pltpu.roll(cw_ref[c], wbuf - 1, 1))


def _s_attn(idx_flat, pt_flat, pool_t, q3, gates3, oc, rows3, win3, cache_win_t, q_pos, nsel, past, n_pages):
    b2 = q3.shape[0]
    wbuf = cache_win_t.shape[3]
    n_top = min(SEL_TOPN, nsel)
    per_b = lambda i, idx, pt: (i, 0, 0)

    def blk_spec(g, k):
        def imap(i, idx, pt):
            j = jnp.minimum(idx[(i * KV_GROUPS + g) * n_top + k], nsel - 2)
            return (pt[i * n_pages + j // 2], 1, 0, 0)
        return pl.BlockSpec((None, 2, KV_W, PAGE_SIZE), imap)

    grid_spec = pltpu.PrefetchScalarGridSpec(
        num_scalar_prefetch=2,
        grid=(b2,),
        in_specs=[blk_spec(g, k) for g in range(KV_GROUPS) for k in range(n_top)] + [
            pl.BlockSpec((None, 1, NSA_W), per_b),
            pl.BlockSpec((None, 1, LANES), per_b),
            pl.BlockSpec((None, NSA_HEADS, LANES), per_b),
            pl.BlockSpec((None, 1, 4 * KV_W), per_b),
            pl.BlockSpec((None, 1, 2 * KV_W), per_b),
            pl.BlockSpec((None, 2, KV_W, wbuf), lambda i, idx, pt: (i, 0, 0, 0)),
        ],
        out_specs=(pl.BlockSpec((None, 1, NSA_W), per_b),
                   pl.BlockSpec((None, 2, KV_W, wbuf), lambda i, idx, pt: (i, 0, 0, 0))),
    )
    return pl.pallas_call(
        functools.partial(_s_attn_kernel, q_pos, nsel, past, wbuf),
        grid_spec=grid_spec,
        out_shape=(jax.ShapeDtypeStruct((b2, 1, NSA_W), _F32),
                   jax.ShapeDtypeStruct((b2, 2, KV_W, wbuf), _F32)),
        compiler_params=_params(("arbitrary",)),
        name="sample_attn",
    )(idx_flat, pt_flat, *([pool_t] * (KV_GROUPS * n_top)), q3, gates3, oc, rows3, win3, cache_win_t)


def _s_ret_kernel(q_ref, k_ref, v_ref, g_ref, gam_ref, st_ref, o_ref, sn_ref):
    rnd = lambda a: a.astype(_MXU).astype(_F32)
    q_t, k_t, v_t, g_t = (r[...].T for r in (q_ref, k_ref, v_ref, g_ref))
    q_t, k_t, v_t = rnd(q_t), rnd(k_t), rnd(v_t)
    nb = q_t.shape[1]
    outs = []
    for hh in range(2):
        hs = slice(hh * HEAD_DIM, (hh + 1) * HEAD_DIM)
        gam = gam_ref[hh]
        v_h = v_t[hs]
        qs = jnp.zeros((HEAD_DIM, nb), _F32)
        for dd in range(HEAD_DIM):
            row = hh * HEAD_DIM + dd
            st = st_ref[hh, dd]
            qs = qs + q_t[row:row + 1] * rnd(st)
            sn_ref[hh, dd] = st * gam + k_t[row:row + 1] * v_h
        att = jnp.sum(q_t[hs] * k_t[hs], axis=0, keepdims=True)
        o = rnd(att) * v_h + qs * gam
        ms = jnp.mean(o * o, axis=0, keepdims=True)
        outs.append(o * lax.rsqrt(ms + RMS_EPS) * jax.nn.silu(g_t[hs]))
    o_ref[...] = jnp.concatenate(outs, axis=0).T.astype(o_ref.dtype)


def _s_ret(ret_s, gam4, st_t):
    b2 = ret_s.shape[0]
    npair = RET_HEADS // 2
    col = lambda k: pl.BlockSpec((b2, LANES), lambda p: (0, k * npair + p))
    pair4 = lambda p: (p, 0, 0, 0)
    return pl.pallas_call(
        _s_ret_kernel,
        grid=(npair,),
        in_specs=[col(0), col(1), col(2), col(3),
                  pl.BlockSpec((None, 2, 1, b2), pair4),
                  pl.BlockSpec((2, HEAD_DIM, HEAD_DIM, b2), pair4)],
        out_specs=(pl.BlockSpec((b2, LANES), lambda p: (0, p)),
                   pl.BlockSpec((2, HEAD_DIM, HEAD_DIM, b2), pair4)),
        out_shape=(jax.ShapeDtypeStruct((b2, RET_W), _MXU),
                   jax.ShapeDtypeStruct((RET_HEADS, HEAD_DIM, HEAD_DIM, b2), _F32)),
        compiler_params=_params(("parallel",)),
        name="sample_ret",
    )(ret_s, ret_s, ret_s, ret_s, gam4, st_t)


def _rope_tables(pos):
    half = HEAD_DIM // 2
    inv = 1.0 / (ROPE_THETA ** (jnp.arange(half, dtype=_F32) / half))
    ang = pos.astype(_F32)[:, None] * inv[None, :]
    cos, sin = jnp.cos(ang), jnp.sin(ang)
    zero = jnp.zeros_like(sin)
    reps = LANES // HEAD_DIM
    cos_t = jnp.tile(cos, (1, 2 * reps))
    sa = jnp.tile(jnp.concatenate([-sin, zero], axis=1), (1, reps))
    sb = jnp.tile(jnp.concatenate([zero, sin], axis=1), (1, reps))
    return cos_t, sa, sb


def _cover_matrix(nsel, nb16, nsel_pad):
    c0 = np.arange(nb16) * CMP_STRIDE
    s0 = np.arange(nsel_pad) * SEL_BLK
    m = (c0[None, :] < s0[:, None] + SEL_BLK) & (c0[None, :] + CMP_BLK > s0[:, None])
    m = m & (np.arange(nsel_pad)[:, None] < nsel)
    return jnp.asarray(m.astype(np.float32), dtype=_MXU)


def _retention_tables():
    lg = jnp.log(1.0 - 2.0 ** (-5.0 - jnp.arange(RET_HEADS, dtype=_F32)))
    c = RET_CHUNK
    i = jnp.arange(c, dtype=_F32)
    diff = i[:, None] - i[None, :]
    causal = diff >= 0
    dmat = jnp.where(causal[None], jnp.exp(jnp.where(causal, diff, 0.0)[None] * lg[:, None, None]), 0.0)
    lane_head = jnp.arange(LANES) // HEAD_DIM
    pair_lg = lg.reshape(RET_HEADS // 2, 2)[:, lane_head]
    rowdec = jnp.exp((i + 1.0)[None, :, None] * pair_lg[:, None, :])
    kdec = jnp.exp((c - 1.0 - i)[None, :, None] * pair_lg[:, None, :])
    sdec = jnp.broadcast_to(jnp.exp(c * pair_lg)[:, :, None], (RET_HEADS // 2, LANES, LANES))
    bmask = (lane_head[:, None] == lane_head[None, :]).astype(_F32)
    return lg, (dmat, rowdec, kdec, sdec, bmask)


def _compress_weights(pos, w1, w2):
    w1t = w1.reshape(CMP_BLK, HEAD_DIM, CMP_HID)
    z1 = jnp.zeros_like(w1t)
    w1bd = jnp.concatenate([jnp.concatenate([w1t, z1], axis=2), jnp.concatenate([z1, w1t], axis=2)], axis=1)
    z2 = jnp.zeros_like(w2)
    w2bd = jnp.concatenate([jnp.concatenate([w2, z2], axis=1), jnp.concatenate([z2, w2], axis=1)], axis=0)
    w1pair = w1bd.reshape(CMP_BLK // 2, 2 * LANES, 2 * CMP_HID)
    return jnp.tile(pos, (1, 2)), w1pair.astype(_MXU), w2bd.astype(_MXU)


def kernel(x_prompt, x_sample, cache_kv, cache_win, state_ret, page_table, ln1, w_in, cmp_pos_k, cmp_w1_k,
           cmp_w2_k, cmp_pos_v, cmp_w1_v, cmp_w2_v, w_out, ln2, w_gate, w_up, w_down, ln_f):
    depth = ln1.shape[0]
    assert depth == 1, "single-layer step"
    b, t, d = x_prompt.shape
    b2, s_s, _ = x_sample.shape
    assert s_s == 1
    n_pages = page_table.shape[1]
    past = n_pages * PAGE_SIZE
    wbuf = cache_win.shape[2]
    assert t % (2 * Q_BLK) == 0 and t >= WINDOW + Q_BLK and (t // SEL_BLK) % 8 == 0
    l = 0

    w = w_in[l]
    o_kv = NSA_W
    o_gt = o_kv + 6 * KV_W
    o_r = o_gt + NSA_HEADS * 3
    w_perm = jnp.concatenate([
        w[:, :o_gt], w[:, o_r:], w[:, o_gt:o_r],
        jnp.zeros((d, LANES - NSA_HEADS * 3), w.dtype)], axis=1).astype(_MXU)
    pos_k, w1k, w2k = _compress_weights(cmp_pos_k[l], cmp_w1_k[l], cmp_w2_k[l])
    pos_v, w1v, w2v = _compress_weights(cmp_pos_v[l], cmp_w1_v[l], cmp_w2_v[l])
    cpos = jnp.stack([pos_k, pos_v])
    cw1 = jnp.stack([w1k, w1v])
    cw2 = jnp.stack([w2k, w2v])
    wo, wg, wu, wd = (a[l].astype(_MXU) for a in (w_out, w_gate, w_up, w_down))
    ln1r, ln2r, lnfr = ln1[l][None, :], ln2[l][None, :], ln_f[None, :]
    lg, ret_tabs = _retention_tables()

    tm = 512 if t % 512 == 0 else 256
    xp2 = x_prompt.reshape(b * t, d)
    q, rows_t, win_t, kk, vt, gates, ret = _proj(xp2, ln1r, w_perm, *_rope_tables(jnp.arange(t)), tm, False)
    kcb, vcbt = _compress_prompt(rows_t, cpos, cw1, cw2)
    nsel_p = t // SEL_BLK
    nb16_p = t // CMP_STRIDE
    eoh = (jnp.arange(t)[:, None] // SEL_BLK == jnp.arange(LANES)[None, :]).astype(_MXU)
    o_nsa = _nsa_prompt(q.reshape(b, t, NSA_W), gates.reshape(b, t, LANES), kcb, vcbt,
                        kk.reshape(b, t, 2 * KV_W), vt, eoh, _cover_matrix(nsel_p, nb16_p, nsel_p),
                        2 if b % 2 == 0 else 1).reshape(b * t, NSA_W)
    o_ret, st_p = _ret_prompt(ret.reshape(b, t, 4 * RET_W), ret_tabs, 4 if b % 4 == 0 else 1)
    o_ret = o_ret.reshape(b * t, RET_W)
    y_prompt = _post(xp2, o_nsa, o_ret, wo, ln2r, wg, wu, wd, lnfr, tm).reshape(b, t, d)
    to_cache = lambda a: jnp.transpose(a.reshape(a.shape[0], a.shape[1], KV_GROUPS, HEAD_DIM, a.shape[3]),
                                       (0, 4, 1, 2, 3))[None]
    kv_prompt = to_cache(rows_t)
    win_keep = min(WINDOW, t)
    win_prompt = to_cache(win_t[:, :, :, t - win_keep:])
    ret_prompt = st_p[None]

    xs2 = x_sample.reshape(b2, d)
    tms = min(tm, b2)
    pos_s = jnp.full((b2,), past, jnp.int32)
    q_s, rows_s, win_s, rows_st, _, _, _, gates_s, ret_s = _proj(xs2, ln1r, w_perm, *_rope_tables(pos_s), tms, True)
    n_pool = cache_kv.shape[1]
    pool_t = jnp.transpose(cache_kv[l], (0, 2, 3, 4, 1)).reshape(n_pool, 4, KV_W, PAGE_SIZE)
    cache_win_t = jnp.transpose(cache_win[l], (0, 2, 3, 4, 1)).reshape(b2, 2, KV_W, wbuf)
    pt_flat = page_table.reshape(-1).astype(jnp.int32)
    kcb_s, vcb_s = _compress_sample(pool_t, pt_flat, b2, n_pages, cpos, cw1, cw2)
    seq_len = past + 1
    nsel_s = -(-seq_len // SEL_BLK)
    nsel_pad = -(-nsel_s // LANES) * LANES
    nb16_s = past // CMP_STRIDE
    q3 = q_s.astype(_F32).reshape(b2, 1, NSA_W)
    idx8, oc = _s_topk(q3, kcb_s, vcb_s, _cover_matrix(nsel_s, nb16_s, nsel_pad), past, nsel_s)
    n_top = min(SEL_TOPN, nsel_s)
    idx_flat = idx8[:, :n_top].reshape(-1)
    o_nsa_s, win_new_t = _s_attn(idx_flat, pt_flat, pool_t, q3, gates_s.reshape(b2, 1, LANES), oc,
                                 rows_s.reshape(b2, 1, 4 * KV_W), win_s.reshape(b2, 1, 2 * KV_W), cache_win_t,
                                 past, nsel_s, past, n_pages)
    o_nsa_s = o_nsa_s.reshape(b2, NSA_W).astype(_MXU)
    gam4 = jnp.broadcast_to(jnp.exp(lg).reshape(RET_HEADS // 2, 2, 1, 1), (RET_HEADS // 2, 2, 1, b2))
    o_ret_s, st_new_t = _s_ret(ret_s, gam4, jnp.transpose(state_ret[l], (1, 2, 3, 0)))
    y_sample = _post(xs2, o_nsa_s, o_ret_s, wo, ln2r, wg, wu, wd, lnfr, tms).reshape(b2, 1, d)
    kv_sample = jnp.transpose(rows_st.reshape(4, KV_GROUPS, HEAD_DIM, b2), (3, 0, 1, 2))[None, :, None]
    win_sample = to_cache(win_new_t)
    ret_sample = jnp.transpose(st_new_t, (3, 0, 1, 2))[None]
    return (y_prompt, y_sample, kv_prompt, kv_sample, win_prompt, win_sample, ret_prompt, ret_sample)
```

```python
import functools

import numpy as np
import jax
import jax.numpy as jnp
from jax import lax
from jax.experimental import pallas as pl
from jax.experimental.pallas import tpu as pltpu

HEAD_DIM = 64
NSA_HEADS = 8
RET_HEADS = 8
KV_GROUPS = 2
HPG = NSA_HEADS // KV_GROUPS
CMP_BLK = 32
CMP_STRIDE = 16
CMP_HID = 4 * HEAD_DIM
SEL_BLK = 64
SEL_TOPN = 16
WINDOW = 512
Q_BLK = 128
RET_CHUNK = 256
PAGE_SIZE = 128
ROPE_THETA = 10000.0
RMS_EPS = 1e-6

LANES = 128
NSA_W = NSA_HEADS * HEAD_DIM
RET_W = RET_HEADS * HEAD_DIM
KV_W = KV_GROUPS * HEAD_DIM
NEG_BIG = -(2.0 ** 100)
LOG2E = 1.4426950408889634
VT_ROWS = KV_W + 16
VMEM_LIMIT = 56 * 1024 * 1024

_MXU = jnp.bfloat16
_F32 = jnp.float32


def _dot(a, b):
    return jnp.dot(a.astype(_MXU), b.astype(_MXU), preferred_element_type=_F32)


def _dot_nt(a, b):
    return lax.dot_general(a.astype(_MXU), b.astype(_MXU), (((1,), (1,)), ((), ())),
                           preferred_element_type=_F32)


def _dot_tn(a, b):
    return lax.dot_general(a.astype(_MXU), b.astype(_MXU), (((0,), (0,)), ((), ())),
                           preferred_element_type=_F32)


def _split3(x):
    hi = x.astype(_MXU)
    r1 = x - hi.astype(_F32)
    mid = r1.astype(_MXU)
    lo = (r1 - mid.astype(_F32)).astype(_MXU)
    return hi, mid, lo


def _params(sem):
    return pltpu.CompilerParams(dimension_semantics=sem, vmem_limit_bytes=VMEM_LIMIT)


_C_Q, _C_KV, _C_RQ, _C_RK, _C_RV, _C_RG, _C_GT = 0, 512, 1280, 1792, 2304, 2816, 3328
_PROJ_COLS = 3456


def _proj_kernel(row_major, x_ref, g_ref, w_ref, cos_ref, sa_ref, sb_ref, q_ref, *out_refs):
    if row_major:
        rows_ref, win_ref = out_refs[:2]
        out_refs = out_refs[2:]
    rows_t_ref, win_t_ref, kk_ref, vt_ref, gates_ref, ret_ref = out_refs
    x = x_ref[...]
    ms = jnp.mean(x * x, axis=-1, keepdims=True)
    h = (x * lax.rsqrt(ms + RMS_EPS) * g_ref[...]).astype(_MXU)
    cos, sa, sb = cos_ref[...], sa_ref[...], sb_ref[...]

    def seg(c0, n):
        return jnp.dot(h, w_ref[:, c0:c0 + n], preferred_element_type=_F32)

    def rope(p):
        return p * cos + pltpu.roll(p, LANES - 32, 1) * sa + pltpu.roll(p, 32, 1) * sb

    def slab(p, s):
        return p[:, s * LANES:(s + 1) * LANES]

    scale = HEAD_DIM ** -0.5
    p = seg(_C_Q, NSA_W)
    for s in range(4):
        q_ref[:, s * LANES:(s + 1) * LANES] = (rope(slab(p, s)) * (scale * LOG2E)).astype(q_ref.dtype)
    p = seg(_C_KV, 6 * KV_W)
    kc, vc = rope(slab(p, 0)), slab(p, 1)
    ks, vs = rope(slab(p, 2)), slab(p, 3)
    kw, vw = rope(slab(p, 4)), slab(p, 5)
    if row_major:
        rows_ref[:, 0:128] = kc
        rows_ref[:, 128:256] = vc
        rows_ref[:, 256:384] = ks
        rows_ref[:, 384:512] = vs
        win_ref[:, 0:128] = kw
        win_ref[:, 128:256] = vw
    kk_ref[:, 0:128] = ks.astype(kk_ref.dtype)
    kk_ref[:, 128:256] = kw.astype(kk_ref.dtype)
    vs_t, vw_t = vs.T, vw.T
    rows_t_ref[0] = kc.T
    rows_t_ref[1] = vc.T
    rows_t_ref[2] = ks.T
    rows_t_ref[3] = vs_t
    win_t_ref[0] = kw.T
    win_t_ref[1] = vw_t
    ones = jnp.ones((VT_ROWS - KV_W, vs_t.shape[1]), vt_ref.dtype)
    vt_ref[0, 0:KV_W] = vs_t.astype(vt_ref.dtype)
    vt_ref[0, KV_W:VT_ROWS] = ones
    vt_ref[1, 0:KV_W] = vw_t.astype(vt_ref.dtype)
    vt_ref[1, KV_W:VT_ROWS] = ones
    p = seg(_C_RQ, RET_W)
    for s in range(4):
        ret_ref[:, s * LANES:(s + 1) * LANES] = rope(slab(p, s))
    p = seg(_C_RK, RET_W)
    for s in range(4):
        ret_ref[:, RET_W + s * LANES:RET_W + (s + 1) * LANES] = rope(slab(p, s)) * scale
    ret_ref[:, 2 * RET_W:3 * RET_W] = seg(_C_RV, RET_W)
    ret_ref[:, 3 * RET_W:4 * RET_W] = seg(_C_RG, RET_W)
    gates_ref[...] = jax.nn.sigmoid(seg(_C_GT, LANES))


def _proj(x2, ln, w_perm, cos, sa, sb, tm, row_major):
    n, d = x2.shape
    tt = cos.shape[0]
    nt = tt // tm
    row = lambda i: (i, 0)
    tab = lambda i: (i % nt, 0)
    const = lambda i: (0, 0)
    tok_t = lambda i: (i // nt, 0, 0, i % nt)
    outs = [
        (jax.ShapeDtypeStruct((n, NSA_W), _MXU), pl.BlockSpec((tm, NSA_W), row)),
    ]
    if row_major:
        outs += [
            (jax.ShapeDtypeStruct((n, 4 * KV_W), _F32), pl.BlockSpec((tm, 4 * KV_W), row)),
            (jax.ShapeDtypeStruct((n, 2 * KV_W), _F32), pl.BlockSpec((tm, 2 * KV_W), row)),
        ]
    outs += [
        (jax.ShapeDtypeStruct((n // tt, 4, KV_W, tt), _F32), pl.BlockSpec((None, 4, KV_W, tm), tok_t)),
        (jax.ShapeDtypeStruct((n // tt, 2, KV_W, tt), _F32), pl.BlockSpec((None, 2, KV_W, tm), tok_t)),
        (jax.ShapeDtypeStruct((n, 2 * KV_W), _MXU), pl.BlockSpec((tm, 2 * KV_W), row)),
        (jax.ShapeDtypeStruct((n // tt, 2, VT_ROWS, tt), _MXU), pl.BlockSpec((None, 2, VT_ROWS, tm), tok_t)),
        (jax.ShapeDtypeStruct((n, LANES), _F32), pl.BlockSpec((tm, LANES), row)),
        (jax.ShapeDtypeStruct((n, 4 * RET_W), _F32), pl.BlockSpec((tm, 4 * RET_W), row)),
    ]
    return pl.pallas_call(
        functools.partial(_proj_kernel, row_major),
        grid=(n // tm,),
        in_specs=[
            pl.BlockSpec((tm, d), row),
            pl.BlockSpec((1, d), const),
            pl.BlockSpec((d, _PROJ_COLS), const),
            pl.BlockSpec((tm, LANES), tab),
            pl.BlockSpec((tm, LANES), tab),
            pl.BlockSpec((tm, LANES), tab),
        ],
        out_specs=tuple(spec for _, spec in outs),
        out_shape=tuple(shape for shape, _ in outs),
        compiler_params=_params(("parallel",)),
        name="proj",
    )(x2, ln, w_perm, cos, sa, sb)


def _post_kernel(x_ref, on_ref, or_ref, wo_ref, g2_ref, wg_ref, wu_ref, wd_ref, gf_ref, y_ref):
    x = x_ref[...]
    mix = (jnp.dot(on_ref[...], wo_ref[0:NSA_W, :], preferred_element_type=_F32)
           + jnp.dot(or_ref[...], wo_ref[NSA_W:NSA_W + RET_W, :], preferred_element_type=_F32))
    x1 = x + mix
    ms = jnp.mean(x1 * x1, axis=-1, keepdims=True)
    h = (x1 * lax.rsqrt(ms + RMS_EPS) * g2_ref[...]).astype(_MXU)
    a = jax.nn.silu(jnp.dot(h, wg_ref[...], preferred_element_type=_F32))
    a = a * jnp.dot(h, wu_ref[...], preferred_element_type=_F32)
    y = x1 + jnp.dot(a.astype(_MXU), wd_ref[...], preferred_element_type=_F32)
    ms = jnp.mean(y * y, axis=-1, keepdims=True)
    y_ref[...] = y * lax.rsqrt(ms + RMS_EPS) * gf_ref[...]


def _post(x2, o_nsa, o_ret, w_out, ln2, w_gate, w_up, w_down, ln_f, tm):
    n, d = x2.shape
    dff = w_gate.shape[1]
    row = lambda i: (i, 0)
    const = lambda i: (0, 0)
    once = dict(pipeline_mode=pl.Buffered(1))
    return pl.pallas_call(
        _post_kernel,
        grid=(n // tm,),
        in_specs=[
            pl.BlockSpec((tm, d), row),
            pl.BlockSpec((tm, NSA_W), row),
            pl.BlockSpec((tm, RET_W), row),
            pl.BlockSpec((d, d), const, **once),
            pl.BlockSpec((1, d), const),
            pl.BlockSpec((d, dff), const, **once),
            pl.BlockSpec((d, dff), const, **once),
            pl.BlockSpec((dff, d), const, **once),
            pl.BlockSpec((1, d), const),
        ],
        out_specs=pl.BlockSpec((tm, d), row),
        out_shape=jax.ShapeDtypeStruct((n, d), _F32),
        compiler_params=_params(("parallel",)),
        name="post",
    )(x2, o_nsa, o_ret, w_out, ln2, w_gate, w_up, w_down, ln_f)


def _fill_token_rows(planes, buf_ref, c):
    for j, plane in enumerate(planes):
        buf_ref[c, j * LANES:(j + 1) * LANES, :] = plane.T


def _compress_rows(buf_ref, c, nb16, pos_ref, w1_ref, w2_ref):
    half = CMP_STRIDE // 2
    a = jnp.zeros((nb16, 2 * CMP_HID), _F32)
    b = jnp.zeros((nb16, 2 * CMP_HID), _F32)
    for tp in range(half):
        t0, t1 = 2 * tp, 2 * tp + 1
        x0 = buf_ref[c, pl.ds(t0, nb16, stride=CMP_STRIDE), :]
        x1 = buf_ref[c, pl.ds(t1, nb16, stride=CMP_STRIDE), :]
        xa = jnp.concatenate([x0 + pos_ref[c, t0:t0 + 1, :], x1 + pos_ref[c, t1:t1 + 1, :]], axis=1)
        xb = jnp.concatenate([x0 + pos_ref[c, CMP_STRIDE + t0:CMP_STRIDE + t0 + 1, :],
                              x1 + pos_ref[c, CMP_STRIDE + t1:CMP_STRIDE + t1 + 1, :]], axis=1)
        a = a + _dot(xa, w1_ref[c, tp])
        b = b + _dot(xb, w1_ref[c, half + tp])
    hid = a + pltpu.roll(b, nb16 - 1, 0)
    return _dot(jax.nn.gelu(hid), w2_ref[c])


def _compress_prompt_kernel(rt_ref, pos_ref, w1_ref, w2_ref, kcb_ref, vcbt_ref, buf_ref):
    nb16 = kcb_ref.shape[0]
    outs = []
    for c in range(2):
        planes = [rt_ref[c, :, j * LANES:(j + 1) * LANES] for j in range(nb16 * CMP_STRIDE // LANES)]
        _fill_token_rows(planes, buf_ref, c)
        outs.append(_compress_rows(buf_ref, c, nb16, pos_ref, w1_ref, w2_ref))
    kcb_ref[...] = outs[0]
    vcbt_ref[...] = outs[1].T


def _compress_prompt(rows_t, pos, w1bd, w2bd):
    b, _, _, t = rows_t.shape
    nb16 = t // CMP_STRIDE
    c3 = lambda i: (0, 0, 0)
    c4 = lambda i: (0, 0, 0, 0)
    out = jax.ShapeDtypeStruct((b, nb16, LANES), _F32)
    return pl.pallas_call(
        _compress_prompt_kernel,
        grid=(b,),
        in_specs=[
            pl.BlockSpec((None, 2, KV_W, t), lambda i: (i, 0, 0, 0)),
            pl.BlockSpec(pos.shape, c3),
            pl.BlockSpec(w1bd.shape, c4),
            pl.BlockSpec(w2bd.shape, c3),
        ],
        out_specs=(pl.BlockSpec((None, nb16, LANES), lambda i: (i, 0, 0)),
                   pl.BlockSpec((None, LANES, nb16), lambda i: (i, 0, 0))),
        out_shape=(out, jax.ShapeDtypeStruct((b, LANES, nb16), _F32)),
        scratch_shapes=[pltpu.VMEM((2, t, LANES), _F32)],
        compiler_params=_params(("parallel",)),
        name="compress_prompt",
    )(rows_t, pos, w1bd, w2bd)


def _compress_sample_kernel(n_pages, pt_ref, *refs):
    page_refs = refs[:n_pages]
    pos_ref, w1_ref, w2_ref, kcb_ref, vcb_ref, buf_ref = refs[n_pages:]
    nb16 = kcb_ref.shape[0]
    for c, out_ref in enumerate((kcb_ref, vcb_ref)):
        _fill_token_rows([page_refs[j][c] for j in range(n_pages)], buf_ref, c)
        out_ref[...] = _compress_rows(buf_ref, c, nb16, pos_ref, w1_ref, w2_ref)


def _compress_sample(pool_t, page_table_flat, b2, n_pages, pos, w1bd, w2bd):
    past = n_pages * PAGE_SIZE
    nb16 = past // CMP_STRIDE
    c3 = lambda i, pt: (0, 0, 0)
    c4 = lambda i, pt: (0, 0, 0, 0)

    def page_spec(j):
        return pl.BlockSpec((None, 2, KV_W, PAGE_SIZE), lambda i, pt: (pt[i * n_pages + j], 0, 0, 0))

    out = jax.ShapeDtypeStruct((b2, nb16, LANES), _F32)
    grid_spec = pltpu.PrefetchScalarGridSpec(
        num_scalar_prefetch=1,
        grid=(b2,),
        in_specs=[page_spec(j) for j in range(n_pages)] + [
            pl.BlockSpec(pos.shape, c3),
            pl.BlockSpec(w1bd.shape, c4),
            pl.BlockSpec(w2bd.shape, c3),
        ],
        out_specs=(pl.BlockSpec((None, nb16, LANES), lambda i, pt: (i, 0, 0)),
                   pl.BlockSpec((None, nb16, LANES), lambda i, pt: (i, 0, 0))),
        scratch_shapes=[pltpu.VMEM((2, past, LANES), _F32)],
    )
    return pl.pallas_call(
        functools.partial(_compress_sample_kernel, n_pages),
        grid_spec=grid_spec,
        out_shape=(out, out),
        compiler_params=_params(("arbitrary",)),
        name="compress_sample",
    )(page_table_flat, *([pool_t] * n_pages), pos, w1bd, w2bd)


def _group_queries(qf, rows):
    lane = lax.broadcasted_iota(jnp.int32, (rows, LANES), 1)
    out = []
    for h in range(NSA_HEADS):
        g = h // HPG
        s = qf[:, (h // 2) * LANES:(h // 2 + 1) * LANES]
        if (h % 2) != g:
            s = pltpu.roll(s, HEAD_DIM, 1)
        out.append(jnp.where((lane >= g * HEAD_DIM) & (lane < (g + 1) * HEAD_DIM), s, 0.0))
    return out


def _masked_softmax(s, mask):
    s = jnp.where(mask, s, -jnp.inf)
    m = jnp.max(s, axis=-1, keepdims=True)
    m = jnp.where(m > -jnp.inf, m, 0.0)
    e = jnp.where(mask, jnp.exp2(s - m), 0.0)
    return e / jnp.maximum(jnp.sum(e, axis=-1, keepdims=True), 1e-30)


def _place_heads(o_heads, rows):
    lane = lax.broadcasted_iota(jnp.int32, (rows, LANES), 1)
    slabs = []
    for k in range(NSA_HEADS // 2):
        pair = []
        for h in (2 * k, 2 * k + 1):
            g = h // HPG
            o = o_heads[h]
            if (h % 2) != g:
                o = pltpu.roll(o, HEAD_DIM, 1)
            pair.append(o)
        slabs.append(jnp.where(lane < HEAD_DIM, pair[0], pair[1]))
    return slabs


def _softmax_keys(s):
    m = jnp.max(s, axis=0, keepdims=True)
    m = jnp.where(m > -jnp.inf, m, 0.0)
    e = jnp.exp2(s - m)
    return e * (1.0 / jnp.maximum(jnp.sum(e, axis=0, keepdims=True), 1e-30))


def _weights_keys(s):
    m = jnp.max(s, axis=0, keepdims=True)
    m = jnp.where(m > -jnp.inf, m, 0.0)
    return jnp.exp2((s - m).astype(_MXU))


def _normalise(o_aug):
    return o_aug[0:KV_W] * (1.0 / jnp.maximum(o_aug[KV_W:KV_W + 1], 1e-30))


def _nsa_prompt_kernel(nsel, q_ref, gates_ref, kcb_ref, vcbt_ref, kk_ref, vt_ref, eoh_ref, cover_ref,
                       o_ref, acc_ref, m_ref, s_ref, qa_ref):
    i = pl.program_id(1)
    nseq = q_ref.shape[0]
    seqs = range(nseq)
    tq = Q_BLK
    qpos_1 = i * tq + lax.broadcasted_iota(jnp.int32, (1, tq), 1)

    def per_head(x):
        return jnp.concatenate([x] * NSA_HEADS, axis=1)

    nb16 = kcb_ref.shape[1]
    blk_end = lax.broadcasted_iota(jnp.int32, (nb16, 1), 0) * CMP_STRIDE + (CMP_BLK - 1)
    bias_c = per_head(jnp.where(blk_end <= qpos_1, 0.0, -jnp.inf))
    q2 = [jnp.concatenate(_group_queries(q_ref[sq].astype(_F32), tq), axis=0).astype(_MXU)
          for sq in seqs]
    p_c = [_softmax_keys(_dot_nt(kcb_ref[sq], q2[sq]) + bias_c) for sq in seqs]
    o_c = [_dot(vcbt_ref[sq], p_c[sq]) for sq in seqs]

    jidx = lax.broadcasted_iota(jnp.int32, (nsel, tq), 0)
    qpos_l = i * tq + lax.broadcasted_iota(jnp.int32, (nsel, tq), 1)
    cur = qpos_l // SEL_BLK
    forced = (jidx == 0) | (jidx == cur) | (jidx == cur - 1)
    valid = jidx * SEL_BLK <= qpos_l
    n_top = min(SEL_TOPN, nsel)
    sub = lax.broadcasted_iota(jnp.int32, (8, tq), 0)
    nv = nsel // 8

    def choice_bias(p_cs, g):
        psum = p_cs[:, (g * HPG) * tq:(g * HPG + 1) * tq]
        for hh in range(1, HPG):
            psum = psum + p_cs[:, (g * HPG + hh) * tq:(g * HPG + hh + 1) * tq]
        imp_t = jnp.zeros((nsel, tq), _F32)
        for part in _split3(psum):
            imp_t = imp_t + jnp.dot(cover_ref[...], part, preferred_element_type=_F32)
        score = jnp.where(forced, jnp.inf, jnp.where(valid, imp_t, -jnp.inf))
        sc_v = [score[8 * v:8 * v + 8] for v in range(nv)]
        rank_v = [jnp.zeros((8, tq), jnp.int32) for _ in range(nv)]
        for jp in range(nsel):
            rowb = jnp.broadcast_to(score[jp:jp + 1, :], (8, tq))
            for v in range(nv):
                if v > jp // 8:
                    beats = jnp.where(rowb >= sc_v[v], 1, 0)
                elif v < jp // 8:
                    beats = jnp.where(rowb > sc_v[v], 1, 0)
                else:
                    beats = jnp.where(sub > (jp % 8), jnp.where(rowb >= sc_v[v], 1, 0),
                                      jnp.where(rowb > sc_v[v], 1, 0))
                rank_v[v] = rank_v[v] + beats
        sel_t = jnp.concatenate([jnp.where(r < n_top, 1.0, 0.0) for r in rank_v], axis=0)
        if nsel < LANES:
            sel_t = jnp.concatenate([sel_t, jnp.ones((LANES - nsel, tq), _F32)], axis=0)
        return ((sel_t.T - 1.0) * (-NEG_BIG)).astype(_MXU)

    for sq in seqs:
        qa_ref[sq, :, 0:LANES] = q2[sq]
        for g in range(KV_GROUPS):
            bias = choice_bias(p_c[sq], g)
            for hh in range(HPG):
                h = g * HPG + hh
                qa_ref[sq, h * tq:(h + 1) * tq, LANES:2 * LANES] = bias

    tk = 2 * tq
    acc_ref[...] = jnp.zeros(acc_ref.shape, _F32)
    m_ref[...] = jnp.full(m_ref.shape, 2.0 * NEG_BIG, _F32)

    def scores(slot, kt):
        k0 = pl.multiple_of(kt * tk, tk)
        eoh = eoh_ref[pl.ds(k0, tk), :]
        for sq in seqs:
            k_aug = jnp.concatenate([kk_ref[sq, pl.ds(k0, tk), 0:LANES], eoh], axis=1)
            s_ref[sq, slot] = _dot_nt(k_aug, qa_ref[sq])

    def consume(slot, kt, causal):
        k0 = pl.multiple_of(kt * tk, tk)
        if causal:
            kpos = k0 + lax.broadcasted_iota(jnp.int32, (tk, 1), 0)
            cbias = per_head(jnp.where(kpos <= qpos_1, 0.0, NEG_BIG))
        for sq in seqs:
            s = s_ref[sq, slot]
            if causal:
                s = s + cbias
            m_old = m_ref[sq, slot]
            m_new = jnp.maximum(m_old, jnp.max(s, axis=0, keepdims=True))
            alpha = jnp.exp2(m_old - m_new)
            p = jnp.exp2((s - m_new).astype(_MXU))
            m_ref[sq, slot] = m_new
            acc_ref[sq, slot] = alpha * acc_ref[sq, slot] + _dot(vt_ref[sq, 0, :, pl.ds(k0, tk)], p)

    last = (i + 2) // 2 - 1
    scores(0, 0)

    def tile_pair(j, carry):
        scores(1, 2 * j + 1)
        consume(0, 2 * j, False)
        scores(0, 2 * j + 2)
        consume(1, 2 * j + 1, False)
        return carry

    lax.fori_loop(0, last // 2, tile_pair, 0)

    @pl.when(last % 2 == 1)
    def _():
        scores(1, last)
        consume(0, last - 1, False)
        consume(1, last, True)

    @pl.when(last % 2 == 0)
    def _():
        consume(0, last, True)

    wk = WINDOW + tq
    start = pl.multiple_of(jnp.maximum(i * tq - WINDOW, 0), tq)
    wpos = start + lax.broadcasted_iota(jnp.int32, (wk, 1), 0)
    bias_w = per_head(jnp.where(wpos <= qpos_1, jnp.where(wpos > qpos_1 - WINDOW, 0.0, -jnp.inf), -jnp.inf))
    for sq in seqs:
        m_all = jnp.maximum(m_ref[sq, 0], m_ref[sq, 1])
        o_s = _normalise(jnp.exp2(m_ref[sq, 0] - m_all) * acc_ref[sq, 0]
                         + jnp.exp2(m_ref[sq, 1] - m_all) * acc_ref[sq, 1])
        e_w = _weights_keys(_dot_nt(kk_ref[sq, pl.ds(start, wk), LANES:2 * LANES], q2[sq]) + bias_w)
        o_w = _normalise(_dot(vt_ref[sq, 1, :, pl.ds(start, wk)], e_w))

        gt_t = gates_ref[sq].T
        o_rows = []
        for h in range(NSA_HEADS):
            g = h // HPG
            rs = slice(g * HEAD_DIM, (g + 1) * HEAD_DIM)
            cs = slice(h * tq, (h + 1) * tq)
            o_rows.append(gt_t[3 * h:3 * h + 1] * o_c[sq][rs, cs] + gt_t[3 * h + 1:3 * h + 2] * o_s[rs, cs]
                          + gt_t[3 * h + 2:3 * h + 3] * o_w[rs, cs])
        o_ref[sq] = jnp.concatenate(o_rows, axis=0).T.astype(o_ref.dtype)


def _nsa_prompt(q3, gates3, kcb, vcbt, kk3, vt4, eoh, cover, nseq):
    b, t, _ = q3.shape
    nq = t // Q_BLK
    nb16 = t // CMP_STRIDE
    nsel = t // SEL_BLK
    qblk = lambda bi, i: (bi, i, 0)
    per_b = lambda bi, i: (bi, 0, 0)
    const = lambda bi, i: (0, 0)
    ncol = NSA_HEADS * Q_BLK
    return pl.pallas_call(
        functools.partial(_nsa_prompt_kernel, nsel),
        grid=(b // nseq, nq),
        in_specs=[
            pl.BlockSpec((nseq, Q_BLK, NSA_W), qblk),
            pl.BlockSpec((nseq, Q_BLK, LANES), qblk),
            pl.BlockSpec((nseq, nb16, LANES), per_b),
            pl.BlockSpec((nseq, LANES, nb16), per_b),
            pl.BlockSpec((nseq, t, 2 * KV_W), per_b),
            pl.BlockSpec((nseq, 2, VT_ROWS, t), lambda bi, i: (bi, 0, 0, 0)),
            pl.BlockSpec((t, LANES), const),
            pl.BlockSpec((nsel, nb16), const),
        ],
        out_specs=pl.BlockSpec((nseq, Q_BLK, NSA_W), qblk),
        out_shape=jax.ShapeDtypeStruct((b, t, NSA_W), _MXU),
        scratch_shapes=[pltpu.VMEM((nseq, 2, VT_ROWS, ncol), _F32),
                        pltpu.VMEM((nseq, 2, 1, ncol), _F32),
                        pltpu.VMEM((nseq, 2, 2 * Q_BLK, ncol), _F32),
                        pltpu.VMEM((nseq, ncol, 2 * LANES), _MXU)],
        compiler_params=_params(("parallel", "arbitrary")),
        name="nsa_prompt",
    )(q3, gates3, kcb, vcbt, kk3, vt4, eoh, cover)


def _ret_prompt_kernel(ret_ref, dmat_ref, rowdec_ref, kdec_ref, sdec_ref, bmask_ref,
                       o_ref, st_ref, s_ref):
    c = pl.program_id(1)
    nc = pl.num_programs(1)
    nseq = ret_ref.shape[0]
    tq = RET_CHUNK

    @pl.when(c == 0)
    def _():
        s_ref[...] = jnp.zeros(s_ref.shape, _F32)

    lane = lax.broadcasted_iota(jnp.int32, (tq, LANES), 1)
    lo = lane < HEAD_DIM
    bmask = bmask_ref[...]
    for pr in range(RET_HEADS // 2):
        c0 = pr * LANES
        dm = jnp.concatenate([dmat_ref[2 * pr], dmat_ref[2 * pr + 1]], axis=0)
        for sq in range(nseq):
            q = ret_ref[sq, :, c0:c0 + LANES]
            k = ret_ref[sq, :, RET_W + c0:RET_W + c0 + LANES]
            v = ret_ref[sq, :, 2 * RET_W + c0:2 * RET_W + c0 + LANES]
            g = ret_ref[sq, :, 3 * RET_W + c0:3 * RET_W + c0 + LANES]
            q2 = jnp.concatenate([jnp.where(lo, q, 0.0), jnp.where(lo, 0.0, q)], axis=0)
            o2 = _dot(_dot_nt(q2, k) * dm, v)
            o = jnp.where(lo, o2[0:tq], o2[tq:2 * tq])
            s_old = s_ref[sq, pr]
            o = o + _dot(q, s_old) * rowdec_ref[pr]
            s_ref[sq, pr] = s_old * sdec_ref[pr] + _dot_tn(k * kdec_ref[pr], v) * bmask
            o_sq = o * o
            s0 = jnp.sum(jnp.where(lo, o_sq, 0.0), axis=-1, keepdims=True)
            s1 = jnp.sum(jnp.where(lo, 0.0, o_sq), axis=-1, keepdims=True)
            ms = jnp.where(lo, s0, s1) * (1.0 / HEAD_DIM)
            o = o * lax.rsqrt(ms + RMS_EPS) * jax.nn.silu(g)
            o_ref[sq, :, c0:c0 + LANES] = o.astype(o_ref.dtype)

    @pl.when(c == nc - 1)
    def _():
        for sq in range(nseq):
            for pr in range(RET_HEADS // 2):
                s_fin = s_ref[sq, pr]
                st_ref[sq, 2 * pr] = s_fin[0:HEAD_DIM, 0:HEAD_DIM]
                st_ref[sq, 2 * pr + 1] = pltpu.roll(s_fin, HEAD_DIM, 1)[HEAD_DIM:2 * HEAD_DIM, 0:HEAD_DIM]


def _ret_prompt(ret3, tabs, nseq):
    b, t, _ = ret3.shape
    nc = t // RET_CHUNK
    blk = lambda bi, i: (bi, i, 0)
    c3 = lambda bi, i: (0, 0, 0)
    dmat, rowdec, kdec, sdec, bmask = tabs
    return pl.pallas_call(
        _ret_prompt_kernel,
        grid=(b // nseq, nc),
        in_specs=[
            pl.BlockSpec((nseq, RET_CHUNK, 4 * RET_W), blk),
            pl.BlockSpec(dmat.shape, c3),
            pl.BlockSpec(rowdec.shape, c3),
            pl.BlockSpec(kdec.shape, c3),
            pl.BlockSpec(sdec.shape, c3),
            pl.BlockSpec(bmask.shape, lambda bi, i: (0, 0)),
        ],
        out_specs=(pl.BlockSpec((nseq, RET_CHUNK, RET_W), blk),
                   pl.BlockSpec((nseq, RET_HEADS, HEAD_DIM, HEAD_DIM), lambda bi, i: (bi, 0, 0, 0))),
        out_shape=(jax.ShapeDtypeStruct((b, t, RET_W), _MXU),
                   jax.ShapeDtypeStruct((b, RET_HEADS, HEAD_DIM, HEAD_DIM), _F32)),
        scratch_shapes=[pltpu.VMEM((nseq, RET_HEADS // 2, LANES, LANES), _F32)],
        compiler_params=_params(("parallel", "arbitrary")),
        name="ret_prompt",
    )(ret3, dmat, rowdec, kdec, sdec, bmask)


def _sample_queries(q_row):
    heads = _group_queries(q_row.astype(_F32), 1)
    row = lax.broadcasted_iota(jnp.int32, (NSA_HEADS, LANES), 0)
    q8 = jnp.zeros((NSA_HEADS, LANES), _F32)
    for h in range(NSA_HEADS):
        q8 = jnp.where(row == h, jnp.broadcast_to(heads[h], (NSA_HEADS, LANES)), q8)
    return q8


def _s_topk_kernel(q_pos, nsel, q_ref, kcb_ref, vcb_ref, cover_ref, idx_ref, oc_ref):
    nseq, nb16, _ = kcb_ref.shape
    nrow = nseq * KV_GROUPS
    blk_end = lax.broadcasted_iota(jnp.int32, (NSA_HEADS, nb16), 1) * CMP_STRIDE + (CMP_BLK - 1)
    row = lax.broadcasted_iota(jnp.int32, (nrow, nb16), 0)
    psum = jnp.zeros((nrow, nb16), _F32)
    for s in range(nseq):
        q8 = _sample_queries(q_ref[s])
        p_c = _masked_softmax(_dot_nt(q8, kcb_ref[s]), blk_end <= q_pos)
        oc_ref[s] = _dot(p_c, vcb_ref[s])
        for g in range(KV_GROUPS):
            acc = p_c[g * HPG:g * HPG + 1]
            for hh in range(1, HPG):
                acc = acc + p_c[g * HPG + hh:g * HPG + hh + 1]
            psum = jnp.where(row == KV_GROUPS * s + g, jnp.broadcast_to(acc, (nrow, nb16)), psum)

    npad = cover_ref.shape[0]
    imp = jnp.zeros((nrow, npad), _F32)
    for part in _split3(psum):
        imp = imp + lax.dot_general(part, cover_ref[...], (((1,), (1,)), ((), ())),
                                    preferred_element_type=_F32)
    j = lax.broadcasted_iota(jnp.int32, (nrow, npad), 1)
    cur = q_pos // SEL_BLK
    forced = (j == 0) | (j == cur) | (j == cur - 1)
    valid = j * SEL_BLK <= q_pos
    score = jnp.where(forced, jnp.inf, jnp.where(valid, imp, -jnp.inf))
    jf = j.astype(_F32)
    alive = jnp.where(j < nsel, 1.0, 0.0)
    lane = lax.broadcasted_iota(jnp.int32, (nrow, LANES), 1)
    idx = jnp.zeros((nrow, LANES), _F32)
    for r in range(min(SEL_TOPN, nsel)):
        live = alive > 0.0
        m = jnp.max(jnp.where(live, score, -jnp.inf), axis=-1, keepdims=True)
        cand = jnp.where(live, jnp.where(score == m, jf, float(npad)), float(npad))
        jmin = jnp.min(cand, axis=-1, keepdims=True)
        idx = jnp.where(lane == r, jmin, idx)
        alive = jnp.where(jf == jmin, 0.0, alive)
    idx_ref[...] = idx.astype(jnp.int32)


def _s_topk(q3, kcb, vcb, cover_s, q_pos, nsel):
    b2 = q3.shape[0]
    nb16 = kcb.shape[1]
    nseq = 16 if b2 % 16 == 0 else (8 if b2 % 8 == 0 else b2)
    per_b = lambda i: (i, 0, 0)
    return pl.pallas_call(
        functools.partial(_s_topk_kernel, q_pos, nsel),
        grid=(b2 // nseq,),
        in_specs=[
            pl.BlockSpec((nseq, 1, NSA_W), per_b),
            pl.BlockSpec((nseq, nb16, LANES), per_b),
            pl.BlockSpec((nseq, nb16, LANES), per_b),
            pl.BlockSpec(cover_s.shape, lambda i: (0, 0)),
        ],
        out_specs=(pl.BlockSpec((nseq * KV_GROUPS, LANES), lambda i: (i, 0)),
                   pl.BlockSpec((nseq, NSA_HEADS, LANES), per_b)),
        out_shape=(jax.ShapeDtypeStruct((b2 * KV_GROUPS, LANES), jnp.int32),
                   jax.ShapeDtypeStruct((b2, NSA_HEADS, LANES), _F32)),
        compiler_params=_params(("parallel",)),
        name="sample_topk",
    )(q3, kcb, vcb, cover_s)


def _s_attn_kernel(q_pos, nsel, past, wbuf, idx_ref, pt_ref, *refs):
    del pt_ref
    n_top = min(SEL_TOPN, nsel)
    nblk = KV_GROUPS * n_top
    blk_refs = refs[:nblk]
    q_ref, gates_ref, oc_ref, rows_ref, win_ref, cw_ref, o_ref, wout_ref = refs[nblk:]
    b = pl.program_id(0)
    q8 = _sample_queries(q_ref[...])
    row1 = lax.broadcasted_iota(jnp.int32, (NSA_HEADS, 1), 0)
    is_g0 = row1 < HPG

    nk = n_top * PAGE_SIZE
    lane_k = lax.broadcasted_iota(jnp.int32, (1, nk), 1)
    kslot = lane_k // PAGE_SIZE
    khalf = (lane_k % PAGE_SIZE) // SEL_BLK
    o_s = None
    for g in range(KV_GROUPS):
        kcat = jnp.concatenate([blk_refs[g * n_top + k][0] for k in range(n_top)], axis=1)
        vcat = jnp.concatenate([blk_refs[g * n_top + k][1] for k in range(n_top)], axis=1)
        s = _dot(q8, kcat)
        bias = jnp.full((1, nk), NEG_BIG, _F32)
        has_new = jnp.zeros((1, 1), _F32)
        for k in range(n_top):
            jk = idx_ref[(b * KV_GROUPS + g) * n_top + k]
            is_new = jk == nsel - 1
            half = jnp.where(is_new, -1, jk % 2)
            bias = jnp.where((kslot == k) & (khalf == half), 0.0, bias)
            has_new = jnp.where(is_new, 1.0, has_new)
        s = s + bias
        k_new = rows_ref[:, 2 * LANES:3 * LANES]
        v_new = rows_ref[:, 3 * LANES:4 * LANES]
        s_new = jnp.sum(q8 * k_new, axis=-1, keepdims=True) + jnp.where(has_new > 0.0, 0.0, NEG_BIG)
        m = jnp.maximum(jnp.max(s, axis=-1, keepdims=True), s_new)
        e = jnp.exp2(s - m)
        e_new = jnp.exp2(s_new - m)
        den = jnp.sum(e, axis=-1, keepdims=True) + e_new
        o_g = (_dot_nt(e, vcat) + e_new * v_new) / den
        o_s = o_g if o_s is None else jnp.where(is_g0, o_s, o_g)

    s_w = _dot(q8, cw_ref[0])
    wpos = (past - wbuf) + lax.broadcasted_iota(jnp.int32, (1, wbuf), 1)
    wmask = (wpos <= q_pos) & (wpos > q_pos - WINDOW) & (wpos >= 0)
    s_w = jnp.where(wmask, s_w, NEG_BIG)
    s_wn = jnp.sum(q8 * win_ref[:, 0:LANES], axis=-1, keepdims=True)
    m = jnp.maximum(jnp.max(s_w, axis=-1, keepdims=True), s_wn)
    e = jnp.exp2(s_w - m)
    e_new = jnp.exp2(s_wn - m)
    den = jnp.sum(e, axis=-1, keepdims=True) + e_new
    o_w = (_dot_nt(e, cw_ref[1]) + e_new * win_ref[:, LANES:2 * LANES]) / den

    gt = jnp.broadcast_to(gates_ref[...], (NSA_HEADS, LANES))
    lane = lax.broadcasted_iota(jnp.int32, (NSA_HEADS, LANES), 1)
    row = lax.broadcasted_iota(jnp.int32, (NSA_HEADS, LANES), 0)
    o = jnp.zeros((NSA_HEADS, LANES), _F32)
    for jb, ob in enumerate((oc_ref[...], o_s, o_w)):
        gcol = jnp.sum(jnp.where(lane == 3 * row + jb, gt, 0.0), axis=-1, keepdims=True)
        o = o + gcol * ob
    heads = [o[h:h + 1] for h in range(NSA_HEADS)]
    for k, s in enumerate(_place_heads(heads, 1)):
        o_ref[:, k * LANES:(k + 1) * LANES] = s.astype(o_ref.dtype)

    eye = (lax.broadcasted_iota(jnp.int32, (KV_W, LANES), 0) == lax.broadcasted_iota(jnp.int32, (KV_W, LANES), 1))
    tok = lax.broadcasted_iota(jnp.int32, (KV_W, wbuf), 1)
    for c in range(2):
        new_row = jnp.broadcast_to(win_ref[:, c * LANES:(c + 1) * LANES], (KV_W, LANES))
        new_col = jnp.sum(jnp.where(eye, new_row, 0.0), axis=-1, keepdims=True)
        wout_ref[c] = jnp.where(tok == wbuf - 1, new_col, pltpu.roll(cw_ref[c], wbuf - 1, 1))


def _s_attn(idx_flat, pt_flat, pool_t, q3, gates3, oc, rows3, win3, cache_win_t, q_pos, nsel, past, n_pages):
    b2 = q3.shape[0]
    wbuf = cache_win_t.shape[3]
    n_top = min(SEL_TOPN, nsel)
    per_b = lambda i, idx, pt: (i, 0, 0)

    def blk_spec(g, k):
        def imap(i, idx, pt):
            j = jnp.minimum(idx[(i * KV_GROUPS + g) * n_top + k], nsel - 2)
            return (pt[i * n_pages + j // 2], 1, 0, 0)
        return pl.BlockSpec((None, 2, KV_W, PAGE_SIZE), imap)

    grid_spec = pltpu.PrefetchScalarGridSpec(
        num_scalar_prefetch=2,
        grid=(b2,),
        in_specs=[blk_spec(g, k) for g in range(KV_GROUPS) for k in range(n_top)] + [
            pl.BlockSpec((None, 1, NSA_W), per_b),
            pl.BlockSpec((None, 1, LANES), per_b),
            pl.BlockSpec((None, NSA_HEADS, LANES), per_b),
            pl.BlockSpec((None, 1, 4 * KV_W), per_b),
            pl.BlockSpec((None, 1, 2 * KV_W), per_b),
            pl.BlockSpec((None, 2, KV_W, wbuf), lambda i, idx, pt: (i, 0, 0, 0)),
        ],
        out_specs=(pl.BlockSpec((None, 1, NSA_W), per_b),
                   pl.BlockSpec((None, 2, KV_W, wbuf), lambda i, idx, pt: (i, 0, 0, 0))),
    )
    return pl.pallas_call(
        functools.partial(_s_attn_kernel, q_pos, nsel, past, wbuf),
        grid_spec=grid_spec,
        out_shape=(jax.ShapeDtypeStruct((b2, 1, NSA_W), _F32),
                   jax.ShapeDtypeStruct((b2, 2, KV_W, wbuf), _F32)),
        compiler_params=_params(("arbitrary",)),
        name="sample_attn",
    )(idx_flat, pt_flat, *([pool_t] * (KV_GROUPS * n_top)), q3, gates3, oc, rows3, win3, cache_win_t)


def _s_ret_kernel(q_ref, k_ref, v_ref, g_ref, gam_ref, st_ref, o_ref, sn_ref):
    rnd = lambda a: a.astype(_MXU).astype(_F32)
    q_t, k_t, v_t, g_t = (r[...].T for r in (q_ref, k_ref, v_ref, g_ref))
    q_t, k_t, v_t = rnd(q_t), rnd(k_t), rnd(v_t)
    nb = q_t.shape[1]
    outs = []
    for hh in range(2):
        hs = slice(hh * HEAD_DIM, (hh + 1) * HEAD_DIM)
        gam = gam_ref[hh]
        v_h = v_t[hs]
        qs = jnp.zeros((HEAD_DIM, nb), _F32)
        for dd in range(HEAD_DIM):
            row = hh * HEAD_DIM + dd
            st = st_ref[hh, dd]
            qs = qs + q_t[row:row + 1] * rnd(st)
            sn_ref[hh, dd] = st * gam + k_t[row:row + 1] * v_h
        att = jnp.sum(q_t[hs] * k_t[hs], axis=0, keepdims=True)
        o = rnd(att) * v_h + qs * gam
        ms = jnp.mean(o * o, axis=0, keepdims=True)
        outs.append(o * lax.rsqrt(ms + RMS_EPS) * jax.nn.silu(g_t[hs]))
    o_ref[...] = jnp.concatenate(outs, axis=0).T.astype(o_ref.dtype)


def _s_ret(ret_s, gam4, st_t):
    b2 = ret_s.shape[0]
    npair = RET_HEADS // 2
    col = lambda k: pl.BlockSpec((b2, LANES), lambda p: (0, k * npair + p))
    pair4 = lambda p: (p, 0, 0, 0)
    return pl.pallas_call(
        _s_ret_kernel,
        grid=(npair,),
        in_specs=[col(0), col(1), col(2), col(3),
                  pl.BlockSpec((None, 2, 1, b2), pair4),
                  pl.BlockSpec((2, HEAD_DIM, HEAD_DIM, b2), pair4)],
        out_specs=(pl.BlockSpec((b2, LANES), lambda p: (0, p)),
                   pl.BlockSpec((2, HEAD_DIM, HEAD_DIM, b2), pair4)),
        out_shape=(jax.ShapeDtypeStruct((b2, RET_W), _MXU),
                   jax.ShapeDtypeStruct((RET_HEADS, HEAD_DIM, HEAD_DIM, b2), _F32)),
        compiler_params=_params(("parallel",)),
        name="sample_ret",
    )(ret_s, ret_s, ret_s, ret_s, gam4, st_t)


def _rope_tables(pos):
    half = HEAD_DIM // 2
    inv = 1.0 / (ROPE_THETA ** (jnp.arange(half, dtype=_F32) / half))
    ang = pos.astype(_F32)[:, None] * inv[None, :]
    cos, sin = jnp.cos(ang), jnp.sin(ang)
    zero = jnp.zeros_like(sin)
    reps = LANES // HEAD_DIM
    cos_t = jnp.tile(cos, (1, 2 * reps))
    sa = jnp.tile(jnp.concatenate([-sin, zero], axis=1), (1, reps))
    sb = jnp.tile(jnp.concatenate([zero, sin], axis=1), (1, reps))
    return cos_t, sa, sb


def _cover_matrix(nsel, nb16, nsel_pad):
    c0 = np.arange(nb16) * CMP_STRIDE
    s0 = np.arange(nsel_pad) * SEL_BLK
    m = (c0[None, :] < s0[:, None] + SEL_BLK) & (c0[None, :] + CMP_BLK > s0[:, None])
    m = m & (np.arange(nsel_pad)[:, None] < nsel)
    return jnp.asarray(m.astype(np.float32), dtype=_MXU)


def _retention_tables():
    lg = jnp.log(1.0 - 2.0 ** (-5.0 - jnp.arange(RET_HEADS, dtype=_F32)))
    c = RET_CHUNK
    i = jnp.arange(c, dtype=_F32)
    diff = i[:, None] - i[None, :]
    causal = diff >= 0
    dmat = jnp.where(causal[None], jnp.exp(jnp.where(causal, diff, 0.0)[None] * lg[:, None, None]), 0.0)
    lane_head = jnp.arange(LANES) // HEAD_DIM
    pair_lg = lg.reshape(RET_HEADS // 2, 2)[:, lane_head]
    rowdec = jnp.exp((i + 1.0)[None, :, None] * pair_lg[:, None, :])
    kdec = jnp.exp((c - 1.0 - i)[None, :, None] * pair_lg[:, None, :])
    sdec = jnp.broadcast_to(jnp.exp(c * pair_lg)[:, :, None], (RET_HEADS // 2, LANES, LANES))
    bmask = (lane_head[:, None] == lane_head[None, :]).astype(_F32)
    return lg, (dmat, rowdec, kdec, sdec, bmask)


def _compress_weights(pos, w1, w2):
    w1t = w1.reshape(CMP_BLK, HEAD_DIM, CMP_HID)
    z1 = jnp.zeros_like(w1t)
    w1bd = jnp.concatenate([jnp.concatenate([w1t, z1], axis=2), jnp.concatenate([z1, w1t], axis=2)], axis=1)
    z2 = jnp.zeros_like(w2)
    w2bd = jnp.concatenate([jnp.concatenate([w2, z2], axis=1), jnp.concatenate([z2, w2], axis=1)], axis=0)
    w1pair = w1bd.reshape(CMP_BLK // 2, 2 * LANES, 2 * CMP_HID)
    return jnp.tile(pos, (1, 2)), w1pair.astype(_MXU), w2bd.astype(_MXU)


def kernel(x_prompt, x_sample, cache_kv, cache_win, state_ret, page_table, ln1, w_in, cmp_pos_k, cmp_w1_k,
           cmp_w2_k, cmp_pos_v, cmp_w1_v, cmp_w2_v, w_out, ln2, w_gate, w_up, w_down, ln_f):
    depth = ln1.shape[0]
    assert depth == 1, "single-layer step"
    b, t, d = x_prompt.shape
    b2, s_s, _ = x_sample.shape
    assert s_s == 1
    n_pages = page_table.shape[1]
    past = n_pages * PAGE_SIZE
    wbuf = cache_win.shape[2]
    assert t % (2 * Q_BLK) == 0 and t >= WINDOW + Q_BLK and (t // SEL_BLK) % 8 == 0
    l = 0

    w = w_in[l]
    o_kv = NSA_W
    o_gt = o_kv + 6 * KV_W
    o_r = o_gt + NSA_HEADS * 3
    w_perm = jnp.concatenate([
        w[:, :o_gt], w[:, o_r:], w[:, o_gt:o_r],
        jnp.zeros((d, LANES - NSA_HEADS * 3), w.dtype)], axis=1).astype(_MXU)
    pos_k, w1k, w2k = _compress_weights(cmp_pos_k[l], cmp_w1_k[l], cmp_w2_k[l])
    pos_v, w1v, w2v = _compress_weights(cmp_pos_v[l], cmp_w1_v[l], cmp_w2_v[l])
    cpos = jnp.stack([pos_k, pos_v])
    cw1 = jnp.stack([w1k, w1v])
    cw2 = jnp.stack([w2k, w2v])
    wo, wg, wu, wd = (a[l].astype(_MXU) for a in (w_out, w_gate, w_up, w_down))
    ln1r, ln2r, lnfr = ln1[l][None, :], ln2[l][None, :], ln_f[None, :]
    lg, ret_tabs = _retention_tables()

    tm = 512 if t % 512 == 0 else 256
    xp2 = x_prompt.reshape(b * t, d)
    q, rows_t, win_t, kk, vt, gates, ret = _proj(xp2, ln1r, w_perm, *_rope_tables(jnp.arange(t)), tm, False)
    kcb, vcbt = _compress_prompt(rows_t, cpos, cw1, cw2)
    nsel_p = t // SEL_BLK
    nb16_p = t // CMP_STRIDE
    eoh = (jnp.arange(t)[:, None] // SEL_BLK == jnp.arange(LANES)[None, :]).astype(_MXU)
    o_nsa = _nsa_prompt(q.reshape(b, t, NSA_W), gates.reshape(b, t, LANES), kcb, vcbt,
                        kk.reshape(b, t, 2 * KV_W), vt, eoh, _cover_matrix(nsel_p, nb16_p, nsel_p),
                        2 if b % 2 == 0 else 1).reshape(b * t, NSA_W)
    o_ret, st_p = _ret_prompt(ret.reshape(b, t, 4 * RET_W), ret_tabs, 4 if b % 4 == 0 else 1)
    o_ret = o_ret.reshape(b * t, RET_W)
    y_prompt = _post(xp2, o_nsa, o_ret, wo, ln2r, wg, wu, wd, lnfr, tm).reshape(b, t, d)
    to_cache = lambda a: jnp.transpose(a.reshape(a.shape[0], a.shape[1], KV_GROUPS, HEAD_DIM, a.shape[3]),
                                       (0, 4, 1, 2, 3))[None]
    kv_prompt = to_cache(rows_t)
    win_keep = min(WINDOW, t)
    win_prompt = to_cache(win_t[:, :, :, t - win_keep:])
    ret_prompt = st_p[None]

    xs2 = x_sample.reshape(b2, d)
    tms = min(tm, b2)
    pos_s = jnp.full((b2,), past, jnp.int32)
    q_s, rows_s, win_s, rows_st, _, _, _, gates_s, ret_s = _proj(xs2, ln1r, w_perm, *_rope_tables(pos_s), tms, True)
    n_pool = cache_kv.shape[1]
    pool_t = jnp.transpose(cache_kv[l], (0, 2, 3, 4, 1)).reshape(n_pool, 4, KV_W, PAGE_SIZE)
    cache_win_t = jnp.transpose(cache_win[l], (0, 2, 3, 4, 1)).reshape(b2, 2, KV_W, wbuf)
    pt_flat = page_table.reshape(-1).astype(jnp.int32)
    kcb_s, vcb_s = _compress_sample(pool_t, pt_flat, b2, n_pages, cpos, cw1, cw2)
    seq_len = past + 1
    nsel_s = -(-seq_len // SEL_BLK)
    nsel_pad = -(-nsel_s // LANES) * LANES
    nb16_s = past // CMP_STRIDE
    q3 = q_s.astype(_F32).reshape(b2, 1, NSA_W)
    idx8, oc = _s_topk(q3, kcb_s, vcb_s, _cover_matrix(nsel_s, nb16_s, nsel_pad), past, nsel_s)
    n_top = min(SEL_TOPN, nsel_s)
    idx_flat = idx8[:, :n_top].reshape(-1)
    o_nsa_s, win_new_t = _s_attn(idx_flat, pt_flat, pool_t, q3, gates_s.reshape(b2, 1, LANES), oc,
                                 rows_s.reshape(b2, 1, 4 * KV_W), win_s.reshape(b2, 1, 2 * KV_W), cache_win_t,
                                 past, nsel_s, past, n_pages)
    o_nsa_s = o_nsa_s.reshape(b2, NSA_W).astype(_MXU)
    gam4 = jnp.broadcast_to(jnp.exp(lg).reshape(RET_HEADS // 2, 2, 1, 1), (RET_HEADS // 2, 2, 1, b2))
    o_ret_s, st_new_t = _s_ret(ret_s, gam4, jnp.transpose(state_ret[l], (1, 2, 3, 0)))
    y_sample = _post(xs2, o_nsa_s, o_ret_s, wo, ln2r, wg, wu, wd, lnfr, tms).reshape(b2, 1, d)
    kv_sample = jnp.transpose(rows_st.reshape(4, KV_GROUPS, HEAD_DIM, b2), (3, 0, 1, 2))[None, :, None]
    win_sample = to_cache(win_new_t)
    ret_sample = jnp.transpose(st_new_t, (3, 0, 1, 2))[None]
    return (y_prompt, y_sample, kv_prompt, kv_sample, win_prompt, win_sample, ret_prompt, ret_sample)
```
